```python
import jax, jax.numpy as jnp
from jax import lax
import numpy as np

D_MODEL = 1024
BATCH = 16
SEQ = 256
DEPTH = 4
DEC_BATCH = 4
DEC_SEQ = 2048
PAST_LEN = 512

GRID_W = 64
N_EVEN = (DEPTH + 1) // 2
N_ODD = DEPTH // 2
MIX_A = D_MODEL // 2
MIX_B = D_MODEL - MIX_A
GLA_HEADS = 4
GLA_DV = MIX_A // GLA_HEADS
GLA_DK = GLA_DV // 2
GLA_QK = GLA_HEADS * GLA_DK
GATE_RANK = 16
GATE_TAU = 16.0
CHUNK = 32
POOL_WINDOWS = (2, 4, 8, 16)
POOL_GROUPS = len(POOL_WINDOWS)
POOL_GW = MIX_B // POOL_GROUPS
MIX_C = D_MODEL // 2
MIX_D = D_MODEL - MIX_C
FOURIER_GROUPS = 4
FOURIER_GW = MIX_C // FOURIER_GROUPS
CONV_W = 3
D_FF = 4 * D_MODEL
EPS = 1e-6
EVEN_IN = 2 * GLA_QK + 2 * MIX_A + 2 * GATE_RANK + MIX_B
ODD_IN = MIX_C + 3 * MIX_D

kernel_name = 'hybrid_diffusion_gla_pool_fourier_conv_step'


def rmsnorm(x, g):
    xf = x.astype(jnp.float32)
    y = xf * lax.rsqrt(jnp.mean(xf * xf, axis=-1, keepdims=True) + EPS)
    return (y * g.astype(jnp.float32)).astype(x.dtype)


def grid_pos_embed(n_tok):
    rows = n_tok // GRID_W
    r = jnp.broadcast_to(jnp.arange(rows, dtype=jnp.float32)[:, None], (rows, GRID_W)).reshape(-1)
    col = jnp.broadcast_to(jnp.arange(GRID_W, dtype=jnp.float32)[None, :], (rows, GRID_W)).reshape(-1)
    quarter = D_MODEL // 4
    freqs = 1.0 / (10000.0 ** (jnp.arange(quarter, dtype=jnp.float32) / quarter))
    ar = r[:, None] * freqs
    ac = col[:, None] * freqs
    return jnp.concatenate([jnp.sin(ar), jnp.cos(ar), jnp.sin(ac), jnp.cos(ac)], axis=-1)


def gla_scan(q, k, v, logf, s0):
    B, L, H, DK = q.shape
    n = L // CHUNK

    def to_chunks(a):
        return jnp.moveaxis(a.reshape(B, n, CHUNK, *a.shape[2:]), 1, 0)

    mask = jnp.tril(jnp.ones((CHUNK, CHUNK), dtype=bool))

    def step(S, inp):
        qc, kc, vc, gc = inp
        b = jnp.cumsum(gc, axis=1)
        b_last = b[:, -1]
        o_inter = jnp.einsum('bihd,bhde->bihe', qc * jnp.exp(b), S)
        diff = b[:, :, None] - b[:, None, :]
        decay = jnp.exp(jnp.where(mask[None, :, :, None, None], diff, -jnp.inf))
        att = jnp.einsum('bihd,bjhd,bijhd->bhij', qc, kc, decay)
        o = o_inter + jnp.einsum('bhij,bjhe->bihe', att, vc)
        S_new = S * jnp.exp(b_last)[..., None] + jnp.einsum(
            'bjhd,bjhe->bhde', kc * jnp.exp(b_last[:, None] - b), vc)
        return S_new, o

    S_fin, o = lax.scan(step, s0, (to_chunks(q), to_chunks(k), to_chunks(v), to_chunks(logf)))
    o = jnp.moveaxis(o, 0, 1).reshape(B, L, H, v.shape[-1])
    return o, S_fin


def multiscale_pool(u, pool_w, pool_s):
    B, L, _ = u.shape
    uf = u.astype(jnp.float32)
    cs = jnp.concatenate([jnp.zeros((B, 1, MIX_B), jnp.float32), jnp.cumsum(uf, axis=1)], axis=1)
    t = jnp.arange(L)
    outs = []
    for gi, w in enumerate(POOL_WINDOWS):
        csg = cs[:, :, gi * POOL_GW:(gi + 1) * POOL_GW]
        lo = jnp.clip(t - w // 2, 0, L)
        hi = jnp.clip(t + w - w // 2, 0, L)
        cnt = (hi - lo).astype(jnp.float32)[None, :, None]
        outs.append((jnp.take(csg, hi, axis=1) - jnp.take(csg, lo, axis=1)) / cnt)
    pooled = (jnp.concatenate(outs, axis=-1) - uf).reshape(B, L, POOL_GROUPS, POOL_GW)
    y = jnp.einsum('blgc,gcd->blgd', pooled, pool_w.astype(jnp.float32)).reshape(B, L, MIX_B)
    return (y * pool_s.astype(jnp.float32)).astype(u.dtype)


def even_mixer(h, s0, w_in, w_a2, b_a2, gla_g, pool_w, pool_s, w_out):
    B, L, _ = h.shape
    f32 = jnp.float32
    p = h @ w_in
    o1 = GLA_QK
    o2 = o1 + GLA_QK
    o3 = o2 + MIX_A
    o4 = o3 + MIX_A
    o5 = o4 + 2 * GATE_RANK
    q = p[..., :o1].astype(f32).reshape(B, L, GLA_HEADS, GLA_DK) * (GLA_DK ** -0.5)
    k = p[..., o1:o2].astype(f32).reshape(B, L, GLA_HEADS, GLA_DK)
    v = p[..., o2:o3].astype(f32).reshape(B, L, GLA_HEADS, GLA_DV)
    g = p[..., o3:o4]
    glr = p[..., o4:o5].astype(f32).reshape(B, L, 2, GATE_RANK)
    u = p[..., o5:]
    pre = jnp.einsum('blzr,zrk->blzk', glr, w_a2.astype(f32)) + b_a2.astype(f32)
    logf = (jax.nn.log_sigmoid(pre) / GATE_TAU).reshape(B, L, 2, GLA_HEADS, GLA_DK)
    s0 = s0.astype(f32)
    o_f, s_f = gla_scan(q, k, v, logf[:, :, 0], s0[:, 0])
    o_b, s_b = gla_scan(jnp.flip(q, 1), jnp.flip(k, 1), jnp.flip(v, 1), jnp.flip(logf[:, :, 1], 1), s0[:, 1])
    o = o_f + jnp.flip(o_b, 1)
    o = o * lax.rsqrt(jnp.mean(o * o, axis=-1, keepdims=True) + EPS) * gla_g.astype(f32)
    o = o.reshape(B, L, MIX_A).astype(h.dtype) * jax.nn.silu(g)
    y_pool = multiscale_pool(u, pool_w, pool_s)
    y = jnp.concatenate([o, y_pool], axis=-1) @ w_out
    return y, jnp.stack([s_f, s_b], axis=1)


def odd_mixer(h, w_in, conv_w, conv_b, w_out):
    B, L, _ = h.shape
    p = h @ w_in
    f = p[..., :MIX_C]
    xin = p[..., MIX_C:MIX_C + MIX_D]
    bg = p[..., MIX_C + MIX_D:MIX_C + 2 * MIX_D]
    cg = p[..., MIX_C + 2 * MIX_D:]
    ff = f.astype(jnp.float32).reshape(B, L, FOURIER_GROUPS, FOURIER_GW)
    four = jnp.real(jnp.fft.fft2(ff, axes=(1, 3), norm='ortho')).reshape(B, L, MIX_C).astype(h.dtype)
    z = cg * xin
    pad = CONV_W // 2
    zp = jnp.pad(z, ((0, 0), (pad, CONV_W - 1 - pad), (0, 0)))
    conv = conv_b
    for j in range(CONV_W):
        conv = conv + zp[:, j:j + L] * conv_w[j]
    y_conv = bg * conv
    return jnp.concatenate([four, y_conv], axis=-1) @ w_out


def sq_relu_mlp(h, w1, w2):
    a = jax.nn.relu(h @ w1)
    return (a * a) @ w2


def trunk(x, cond, gla_init, norm_g, final_g, w_ada, b_ada, w_mlp1, w_mlp2,
          w_in_even, w_a2, b_a2, gla_norm_g, pool_w, pool_s, w_out_even,
          w_in_odd, conv_w, conv_b, w_out_odd):
    states = []
    for l in range(DEPTH):
        m = jax.nn.silu(cond) @ w_ada[l] + b_ada[l]
        sh1, sc1, g1, sh2, sc2, g2 = jnp.split(m, 6, axis=-1)
        h = rmsnorm(x, norm_g[l, 0]) * (1 + sc1) + sh1
        j = l // 2
        if l % 2 == 0:
            y, s = even_mixer(h, gla_init[:, j], w_in_even[j], w_a2[j], b_a2[j], gla_norm_g[j],
                              pool_w[j], pool_s[j], w_out_even[j])
            states.append(s)
        else:
            y = odd_mixer(h, w_in_odd[j], conv_w[j], conv_b[j], w_out_odd[j])
        x = x + g1 * y
        h = rmsnorm(x, norm_g[l, 1]) * (1 + sc2) + sh2
        x = x + g2 * sq_relu_mlp(h, w_mlp1[l], w_mlp2[l])
    return rmsnorm(x, final_g), jnp.stack(states, axis=1)


def setup_inputs(seed: int = 0) -> dict:
    key = jax.random.key(seed)
    ks = jax.random.split(key, 24)
    f32 = jnp.float32

    def nrm(k, shape, std):
        return jax.random.normal(k, shape, f32) * std

    return {
        'x_prompt': nrm(ks[0], (BATCH, SEQ, D_MODEL), 1.0),
        'x_sample': nrm(ks[1], (DEC_BATCH, DEC_SEQ, D_MODEL), 1.0),
        'state_gla': nrm(ks[2], (DEC_BATCH, N_EVEN, 2, GLA_HEADS, GLA_DK, GLA_DV), 2.0),
        'c': nrm(ks[3], (DEC_BATCH, D_MODEL), 1.0),
        'c_ctx': nrm(ks[4], (D_MODEL,), 1.0),
        'norm_g': 1.0 + nrm(ks[5], (DEPTH, 2, D_MODEL), 0.05),
        'final_g': 1.0 + nrm(ks[6], (D_MODEL,), 0.05),
        'w_ada': nrm(ks[7], (DEPTH, D_MODEL, 6 * D_MODEL), 0.5 * D_MODEL ** -0.5),
        'b_ada': nrm(ks[8], (DEPTH, 6 * D_MODEL), 0.02),
        'w_mlp1': nrm(ks[9], (DEPTH, D_MODEL, D_FF), D_MODEL ** -0.5),
        'w_mlp2': nrm(ks[10], (DEPTH, D_FF, D_MODEL), D_FF ** -0.5),
        'w_in_even': nrm(ks[11], (N_EVEN, D_MODEL, EVEN_IN), D_MODEL ** -0.5),
        'w_a2': nrm(ks[12], (N_EVEN, 2, GATE_RANK, GLA_QK), GATE_RANK ** -0.5),
        'b_a2': nrm(ks[13], (N_EVEN, 2, GLA_QK), 0.1),
        'gla_norm_g': 1.0 + nrm(ks[14], (N_EVEN, GLA_DV), 0.05),
        'pool_w': nrm(ks[15], (N_EVEN, POOL_GROUPS, POOL_GW, POOL_GW), POOL_GW ** -0.5),
        'pool_s': 1.0 + nrm(ks[16], (N_EVEN, MIX_B), 0.05),
        'w_out_even': nrm(ks[17], (N_EVEN, D_MODEL, D_MODEL), D_MODEL ** -0.5),
        'w_in_odd': nrm(ks[18], (N_ODD, D_MODEL, ODD_IN), D_MODEL ** -0.5),
        'conv_w': nrm(ks[19], (N_ODD, CONV_W, MIX_D), CONV_W ** -0.5),
        'conv_b': nrm(ks[20], (N_ODD, MIX_D), 0.02),
        'w_out_odd': nrm(ks[21], (N_ODD, D_MODEL, D_MODEL), D_MODEL ** -0.5),
    }


def reference(x_prompt, x_sample, state_gla, c, c_ctx, norm_g, final_g, w_ada, b_ada, w_mlp1, w_mlp2,
              w_in_even, w_a2, b_a2, gla_norm_g, pool_w, pool_s, w_out_even,
              w_in_odd, conv_w, conv_b, w_out_odd):
    zero_state = jnp.zeros((x_prompt.shape[0], N_EVEN, 2, GLA_HEADS, GLA_DK, GLA_DV), jnp.float32)
    y_prompt, new_state_gla = trunk(
        x_prompt, c_ctx[None, None, :], zero_state, norm_g, final_g, w_ada, b_ada, w_mlp1, w_mlp2,
        w_in_even, w_a2, b_a2, gla_norm_g, pool_w, pool_s, w_out_even, w_in_odd, conv_w, conv_b, w_out_odd)
    n_lat = x_sample.shape[1]
    x_lat = x_sample + grid_pos_embed(n_lat).astype(x_sample.dtype)[None]
    y_sample, _ = trunk(
        x_lat, c[:, None, :], state_gla, norm_g, final_g, w_ada, b_ada, w_mlp1, w_mlp2,
        w_in_even, w_a2, b_a2, gla_norm_g, pool_w, pool_s, w_out_even, w_in_odd, conv_w, conv_b, w_out_odd)
    return (y_prompt, y_sample, new_state_gla.astype(x_prompt.dtype))
```

```python
import functools
import math

import numpy as np
import jax
import jax.numpy as jnp
from jax import lax
from jax.experimental import pallas as pl
from jax.experimental.pallas import tpu as pltpu

D_MODEL = 1024
DEPTH = 4
N_EVEN = 2
GRID_W = 64
MIX_A = 512
MIX_B = 512
GLA_HEADS = 4
GLA_DV = 128
GLA_DK = 64
GLA_QK = 256
LOG2_DK = 6
LOG2_DV = 7
GATE_RANK = 16
GATE_TAU = 16.0
POOL_WINDOWS = (2, 4, 8, 16)
POOL_GW = 128
MIX_C = 512
MIX_D = 512
FOURIER_GW = 128
D_FF = 4096
EPS = 1e-6

LANES = 128
TOKEN_TILE = 512
GLA_CHUNK = 256
EVEN_IN_PAD = 2048 + LANES
VMEM_LIMIT = 56 * 1024 * 1024

F32 = jnp.float32
BF16 = jnp.bfloat16


def _dot(a, b):
    return jnp.dot(a, b, preferred_element_type=F32)


def _dot_nt(a, b):
    return lax.dot_general(a, b, (((1,), (1,)), ((), ())), preferred_element_type=F32)


def _split_dot(m, x):
    hi = x.astype(BF16)
    lo = (x - hi.astype(F32)).astype(BF16)
    return _dot(m, hi) + _dot(m, lo)


def _silu(x):
    return x * (1.0 / (1.0 + jnp.exp(-x)))


def _rmsnorm(x, g):
    ms = jnp.mean(x * x, axis=-1, keepdims=True)
    return (x * lax.rsqrt(ms + EPS)) * g


def _params(n_axes):
    return pltpu.CompilerParams(dimension_semantics=("arbitrary",) * n_axes, vmem_limit_bytes=VMEM_LIMIT)


def _resident(shape):
    zeros = (0,) * len(shape)
    return pl.BlockSpec(shape, lambda *_: zeros, pipeline_mode=pl.Buffered(1))


ADA_COLS = 1536


def _ada_kernel(cond_ref, w_ref, b_ref, out_ref):
    s = _silu(cond_ref[...]).astype(BF16)
    out_ref[0] = _dot(s, w_ref[0].astype(BF16)) + b_ref[0]


def _ada(cond8, w_ada, b_ada):
    rows = cond8.shape[0]
    n_out = w_ada.shape[-1]
    return pl.pallas_call(
        _ada_kernel,
        grid=(DEPTH, n_out // ADA_COLS),
        in_specs=[
            pl.BlockSpec((rows, D_MODEL), lambda l, j: (0, 0)),
            pl.BlockSpec((1, D_MODEL, ADA_COLS), lambda l, j: (l, 0, j)),
            pl.BlockSpec((1, 1, ADA_COLS), lambda l, j: (l, 0, j)),
        ],
        out_specs=pl.BlockSpec((1, rows, ADA_COLS), lambda l, j: (l, 0, j)),
        out_shape=jax.ShapeDtypeStruct((DEPTH, rows, n_out), F32),
        compiler_params=_params(2),
        name="ada_mod",
    )(cond8, w_ada, b_ada.reshape(DEPTH, 1, n_out))


def _modulated_norm(x, mod_ref, g_ref, which):
    shift = mod_ref[0, :, (3 * which) * D_MODEL:(3 * which + 1) * D_MODEL]
    scale = mod_ref[0, :, (3 * which + 1) * D_MODEL:(3 * which + 2) * D_MODEL]
    return _rmsnorm(x, g_ref[...]) * (1.0 + scale) + shift


def _in_even_kernel(x_ref, mod_ref, g_ref, w_ref, wgate_ref, bgate_ref,
                    qk_ref, v_ref, gate_ref, u_ref, lf_ref):
    h = _modulated_norm(x_ref[...], mod_ref, g_ref, 0).astype(BF16)
    p = _dot(h, w_ref[...])
    qk_ref[:, :GLA_QK] = p[:, :GLA_QK] * (GLA_DK ** -0.5)
    qk_ref[:, GLA_QK:] = p[:, GLA_QK:2 * GLA_QK]
    v_ref[...] = p[:, 512:1024].astype(BF16)
    gate_ref[...] = p[:, 1024:1536]
    u_ref[...] = p[:, 1536:2048]
    pre = _dot(p[:, 2048:].astype(BF16), wgate_ref[...]) + bgate_ref[...]
    log_sig = jnp.minimum(pre, 0.0) - jnp.log(1.0 + jnp.exp(-jnp.abs(pre)))
    lf_ref[...] = log_sig / GATE_TAU


def _in_odd_kernel(x_ref, mod_ref, g_ref, w_ref, f_ref, z_ref, bg_ref):
    h = _modulated_norm(x_ref[...], mod_ref, g_ref, 0).astype(BF16)
    p = _dot(h, w_ref[...])
    f_ref[...] = p[:, :MIX_C].astype(BF16)
    z_ref[...] = p[:, MIX_C + 2 * MIX_D:] * p[:, MIX_C:MIX_C + MIX_D]
    bg_ref[...] = p[:, MIX_C + MIX_D:MIX_C + 2 * MIX_D]


def _mod_spec(seq_len, shared):
    if shared:
        return pl.BlockSpec((1, 1, 6 * D_MODEL), lambda i: (0, 0, 0))
    return pl.BlockSpec((1, 1, 6 * D_MODEL), lambda i: (i * TOKEN_TILE // seq_len, 0, 0))


def _tile_spec(cols):
    return pl.BlockSpec((TOKEN_TILE, cols), lambda i: (i, 0))


def _in_even(x, mod, norm_g, w_in, wgate, bgate, seq_len, shared):
    n = x.shape[0]
    outs = [(512, F32), (512, BF16), (512, F32), (512, F32), (512, F32)]
    return pl.pallas_call(
        _in_even_kernel,
        grid=(n // TOKEN_TILE,),
        in_specs=[
            _tile_spec(D_MODEL), _mod_spec(seq_len, shared), _resident((1, D_MODEL)),
            _resident((D_MODEL, EVEN_IN_PAD)), _resident((LANES, 2 * GLA_QK)), _resident((1, 2 * GLA_QK)),
        ],
        out_specs=[_tile_spec(c) for c, _ in outs],
        out_shape=[jax.ShapeDtypeStruct((n, c), dt) for c, dt in outs],
        compiler_params=_params(1),
        name="in_even",
    )(x, mod, norm_g, w_in, wgate, bgate)


def _in_odd(x, mod, norm_g, w_in, seq_len, shared):
    n = x.shape[0]
    outs = [(MIX_C, BF16), (MIX_D, F32), (MIX_D, F32)]
    return pl.pallas_call(
        _in_odd_kernel,
        grid=(n // TOKEN_TILE,),
        in_specs=[
            _tile_spec(D_MODEL), _mod_spec(seq_len, shared), _resident((1, D_MODEL)),
            _resident((D_MODEL, MIX_C + 3 * MIX_D)),
        ],
        out_specs=[_tile_spec(c) for c, _ in outs],
        out_shape=[jax.ShapeDtypeStruct((n, c), dt) for c, dt in outs],
        compiler_params=_params(1),
        name="in_odd",
    )(x, mod, norm_g, w_in)


FF_BLOCK = 1024


def _post_kernel(x_ref, a_ref, b_ref, mod_ref, g_ref, wo_ref, w1_ref, w2_ref, fg_ref, out_ref, *, final):
    half = D_MODEL // 2
    y = _dot(a_ref[...], wo_ref[:half, :]) + _dot(b_ref[...], wo_ref[half:, :])
    gate1 = mod_ref[0, :, 2 * D_MODEL:3 * D_MODEL]
    gate2 = mod_ref[0, :, 5 * D_MODEL:6 * D_MODEL]
    x1 = x_ref[...] + gate1 * y
    h = _modulated_norm(x1, mod_ref, g_ref, 1).astype(BF16)
    acc = jnp.zeros_like(x1)
    for c in range(D_FF // FF_BLOCK):
        a = jnp.maximum(_dot(h, w1_ref[:, c * FF_BLOCK:(c + 1) * FF_BLOCK]), 0.0)
        acc = acc + _dot((a * a).astype(BF16), w2_ref[c * FF_BLOCK:(c + 1) * FF_BLOCK, :])
    x2 = x1 + gate2 * acc
    if final:
        x2 = _rmsnorm(x2, fg_ref[...])
    out_ref[...] = x2


def _post(x, a, b, mod, norm_g, w_out, w1, w2, final_g, seq_len, shared, final):
    n = x.shape[0]
    return pl.pallas_call(
        functools.partial(_post_kernel, final=final),
        grid=(n // TOKEN_TILE,),
        in_specs=[
            _tile_spec(D_MODEL), _tile_spec(D_MODEL // 2), _tile_spec(D_MODEL // 2),
            _mod_spec(seq_len, shared), _resident((1, D_MODEL)),
            _resident((D_MODEL, D_MODEL)), _resident((D_MODEL, D_FF)), _resident((D_FF, D_MODEL)),
            _resident((1, D_MODEL)),
        ],
        out_specs=_tile_spec(D_MODEL),
        out_shape=jax.ShapeDtypeStruct((n, D_MODEL), F32),
        compiler_params=_params(1),
        name="post_final" if final else "post",
    )(x, a, b, mod, norm_g, w_out, w1, w2, final_g)


def _head_block_mask():
    r = lax.broadcasted_iota(jnp.int32, (GLA_QK, MIX_A), 0) >> LOG2_DK
    c = lax.broadcasted_iota(jnp.int32, (GLA_QK, MIX_A), 1) >> LOG2_DV
    return jnp.where(r == c, 1.0, 0.0).astype(F32)


def _gla_kernel(*refs, n_chunks, has_state, want_state):
    qk_ref, v_ref, lf_ref, gate_ref, gn_ref = refs[:5]
    refs = refs[5:]
    s0_ref = None
    if has_state:
        s0_ref, refs = refs[0], refs[1:]
    o_ref, refs = refs[0], refs[1:]
    sout_ref = None
    if want_state:
        sout_ref, refs = refs[0], refs[1:]
    state_ref, fwd_ref = refs

    C = GLA_CHUNK
    direction = pl.program_id(1)
    step = pl.program_id(2)
    backward = direction == 1

    @pl.when(step == 0)
    def _():
        if has_state:
            s0 = s0_ref[0, 0]
            state_ref[...] = jnp.concatenate([s0] * GLA_HEADS, axis=1) * _head_block_mask()
        else:
            state_ref[...] = jnp.zeros_like(state_ref)

    q = qk_ref[:, :GLA_QK]
    k = qk_ref[:, GLA_QK:]
    v = v_ref[...]
    lf = jnp.where(backward, lf_ref[:, GLA_QK:], lf_ref[:, :GLA_QK])

    row = lax.broadcasted_iota(jnp.int32, (C, C), 0)
    col = lax.broadcasted_iota(jnp.int32, (C, C), 1)
    sign = 1 - 2 * direction
    causal = (row - col) * sign >= 0
    cum = _split_dot(jnp.where(causal, 1.0, 0.0).astype(BF16), lf)
    total_row = jnp.where(backward, cum[0:1, :], cum[C - 1:C, :])
    mid = cum[C // 2:C // 2 + 1, :]

    state = state_ref[...]
    q_in = (q * jnp.exp(cum)).astype(BF16)
    o = _dot(q_in, state.astype(BF16))

    q_c = (q * jnp.exp(cum - mid)).astype(BF16)
    k_c = k * jnp.exp(mid - cum)
    lane_head = lax.broadcasted_iota(jnp.int32, (C, GLA_QK), 1) >> LOG2_DK
    intra = []
    for h in range(GLA_HEADS):
        k_h = jnp.where(lane_head == h, k_c, 0.0).astype(BF16)
        att = jnp.where(causal, _dot_nt(q_c, k_h), 0.0).astype(BF16)
        intra.append(_dot(att, v[:, h * GLA_DV:(h + 1) * GLA_DV]))
    o = o + jnp.concatenate(intra, axis=1)

    k_out = (k * jnp.exp(total_row - cum)).T.astype(BF16)
    decay = jnp.exp(jnp.sum(lf.T, axis=1, keepdims=True))
    mask = _head_block_mask()
    new_state = state * decay + _dot(k_out, v) * mask
    state_ref[...] = new_state

    chunk = pl.multiple_of(jnp.where(backward, n_chunks - 1 - step, step) * C, C)

    @pl.when(direction == 0)
    def _():
        fwd_ref[pl.ds(chunk, C), :] = o

    @pl.when(backward)
    def _():
        both = fwd_ref[pl.ds(chunk, C), :] + o
        normed = []
        for h in range(GLA_HEADS):
            o_h = both[:, h * GLA_DV:(h + 1) * GLA_DV]
            ms = jnp.mean(o_h * o_h, axis=-1, keepdims=True)
            normed.append(o_h * lax.rsqrt(ms + EPS) * gn_ref[...])
        o_ref[...] = (jnp.concatenate(normed, axis=1) * _silu(gate_ref[...])).astype(BF16)

    if want_state:
        @pl.when(step == n_chunks - 1)
        def _():
            s = new_state[:, :GLA_DV]
            for h in range(1, GLA_HEADS):
                s = s + new_state[:, h * GLA_DV:(h + 1) * GLA_DV]
            sout_ref[0, 0] = s


def _gla(qk, v, lf, gate, gn, s0, batch, seq_len, want_state):
    n = qk.shape[0]
    n_chunks = seq_len // GLA_CHUNK
    has_state = s0 is not None

    def chunk_map(b, d, c):
        return (b * n_chunks + jnp.where(d == 0, c, n_chunks - 1 - c), 0)

    def out_map(b, d, c):
        return (b * n_chunks + jnp.where(d == 0, n_chunks - 1, n_chunks - 1 - c), 0)

    def state_map(b, d, c):
        return (b, d, 0, 0)

    in_specs = [
        pl.BlockSpec((GLA_CHUNK, 2 * GLA_QK), chunk_map),
        pl.BlockSpec((GLA_CHUNK, MIX_A), chunk_map),
        pl.BlockSpec((GLA_CHUNK, 2 * GLA_QK), chunk_map),
        pl.BlockSpec((GLA_CHUNK, MIX_A), chunk_map),
        pl.BlockSpec((1, GLA_DV), lambda b, d, c: (0, 0)),
    ]
    args = [qk, v, lf, gate, gn]
    if has_state:
        in_specs.append(pl.BlockSpec((1, 1, GLA_QK, GLA_DV), state_map))
        args.append(s0)
    out_specs = [pl.BlockSpec((GLA_CHUNK, MIX_A), out_map)]
    out_shape = [jax.ShapeDtypeStruct((n, MIX_A), BF16)]
    if want_state:
        out_specs.append(pl.BlockSpec((1, 1, GLA_QK, GLA_DV), state_map))
        out_shape.append(jax.ShapeDtypeStruct((batch, 2, GLA_QK, GLA_DV), F32))
    outs = pl.pallas_call(
        functools.partial(_gla_kernel, n_chunks=n_chunks, has_state=has_state, want_state=want_state),
        grid=(batch, 2, n_chunks),
        in_specs=in_specs,
        out_specs=out_specs,
        out_shape=out_shape,
        scratch_shapes=[pltpu.VMEM((GLA_QK, MIX_A), F32), pltpu.VMEM((seq_len, MIX_A), F32)],
        compiler_params=_params(3),
        name="gla",
    )(*args)
    return (outs[0], outs[1]) if want_state else (outs[0], None)


def _shift_down(x, s):
    n = x.shape[0]
    t = lax.broadcasted_iota(jnp.int32, x.shape, 0)
    return jnp.where(t >= s, pltpu.roll(x, s, axis=0), 0.0)


def _shift_up(x, s):
    n = x.shape[0]
    t = lax.broadcasted_iota(jnp.int32, x.shape, 0)
    return jnp.where(t < n - s, pltpu.roll(x, n - s, axis=0), 0.0)


def _pool_kernel(u_ref, w_ref, s_ref, out_ref):
    n = u_ref.shape[0]
    t = lax.broadcasted_iota(jnp.int32, (n, POOL_GW), 0)
    for gi, w in enumerate(POOL_WINDOWS):
        sl = slice(gi * POOL_GW, (gi + 1) * POOL_GW)
        u = u_ref[:, sl]
        half = w // 2
        trail, lead, m = u, u, 1
        while m < half:
            trail = trail + _shift_down(trail, m)
            lead = lead + _shift_up(lead, m)
            m *= 2
        window = _shift_down(trail, 1) + lead
        cnt = (jnp.minimum(t + (w - half), n) - jnp.maximum(t - half, 0)).astype(F32)
        pooled = window / cnt - u
        y = _dot(pooled.astype(BF16), w_ref[gi])
        out_ref[:, sl] = (y * s_ref[:, sl]).astype(BF16)


def _pool(u, pool_w, pool_s, batch, seq_len):
    n = u.shape[0]
    return pl.pallas_call(
        _pool_kernel,
        grid=(batch,),
        in_specs=[
            pl.BlockSpec((seq_len, MIX_B), lambda b: (b, 0)),
            _resident((len(POOL_WINDOWS), POOL_GW, POOL_GW)),
            _resident((1, MIX_B)),
        ],
        out_specs=pl.BlockSpec((seq_len, MIX_B), lambda b: (b, 0)),
        out_shape=jax.ShapeDtypeStruct((n, MIX_B), BF16),
        compiler_params=_params(1),
        name="pool",
    )(u, pool_w, pool_s)


def _conv_kernel(z_ref, bg_ref, w_ref, b_ref, out_ref):
    z = z_ref[...]
    conv = b_ref[...] + _shift_down(z, 1) * w_ref[0:1, :]
    conv = conv + z * w_ref[1:2, :]
    conv = conv + _shift_up(z, 1) * w_ref[2:3, :]
    out_ref[...] = (bg_ref[...] * conv).astype(BF16)


def _conv(z, bg, conv_w, conv_b, batch, seq_len):
    n = z.shape[0]
    seq_spec = pl.BlockSpec((seq_len, MIX_D), lambda b: (b, 0))
    return pl.pallas_call(
        _conv_kernel,
        grid=(batch,),
        in_specs=[seq_spec, seq_spec, _resident(conv_w.shape), _resident((1, MIX_D))],
        out_specs=seq_spec,
        out_shape=jax.ShapeDtypeStruct((n, MIX_D), BF16),
        compiler_params=_params(1),
        name="gated_conv",
    )(z, bg, conv_w, conv_b)


DFT_ROWS = 256


def _dft_tables(n):
    idx = jnp.arange(n, dtype=jnp.int32)
    ang = ((idx[:, None] * idx[None, :]) % n).astype(F32) * (2.0 * math.pi / n)
    return jnp.cos(ang), jnp.sin(ang)


def _fourier_kernel(f_ref, cs_ref, cl_ref, sl_ref, out_ref, xc_ref, xs_ref, *, scale):
    n = f_ref.shape[0]
    for g in range(MIX_C // FOURIER_GW):
        sl = slice(g * FOURIER_GW, (g + 1) * FOURIER_GW)
        x = _dot(f_ref[:, sl], cs_ref[...])
        xc_ref[:, sl] = x[:, :FOURIER_GW].astype(BF16)
        xs_ref[:, sl] = x[:, FOURIER_GW:].astype(BF16)

    def body(r, carry):
        rows = pl.ds(pl.multiple_of(r * DFT_ROWS, DFT_ROWS), DFT_ROWS)
        re = _dot(cl_ref[rows, :], xc_ref[...]) - _dot(sl_ref[rows, :], xs_ref[...])
        out_ref[rows, :] = (re * scale).astype(BF16)
        return carry

    lax.fori_loop(0, n // DFT_ROWS, body, 0)


def _fourier_tables(seq_len):
    cc, sc = _dft_tables(FOURIER_GW)
    cl, sl = _dft_tables(seq_len)
    return jnp.concatenate([cc, sc], axis=1).astype(BF16), cl.astype(BF16), sl.astype(BF16)


def _fourier(f, tables, batch, seq_len):
    n = f.shape[0]
    scale = 1.0 / math.sqrt(seq_len * FOURIER_GW)
    seq_spec = pl.BlockSpec((seq_len, MIX_C), lambda b: (b, 0))
    return pl.pallas_call(
        functools.partial(_fourier_kernel, scale=scale),
        grid=(batch,),
        in_specs=[seq_spec, _resident((FOURIER_GW, 2 * FOURIER_GW)),
                  _resident((seq_len, seq_len)), _resident((seq_len, seq_len))],
        out_specs=seq_spec,
        out_shape=jax.ShapeDtypeStruct((n, MIX_C), BF16),
        scratch_shapes=[pltpu.VMEM((seq_len, MIX_C), BF16), pltpu.VMEM((seq_len, MIX_C), BF16)],
        compiler_params=_params(1),
        name="fourier",
    )(f, *tables)


def _grid_pos_embed(n_tok):
    rows = n_tok // GRID_W
    r = jnp.broadcast_to(jnp.arange(rows, dtype=F32)[:, None], (rows, GRID_W)).reshape(-1)
    col = jnp.broadcast_to(jnp.arange(GRID_W, dtype=F32)[None, :], (rows, GRID_W)).reshape(-1)
    quarter = D_MODEL // 4
    freqs = 1.0 / (10000.0 ** (jnp.arange(quarter, dtype=F32) / quarter))
    ar = r[:, None] * freqs
    ac = col[:, None] * freqs
    return jnp.concatenate([jnp.sin(ar), jnp.cos(ar), jnp.sin(ac), jnp.cos(ac)], axis=-1)


def _prep_weights(norm_g, final_g, w_mlp1, w_mlp2, w_in_even, w_a2, b_a2, gla_norm_g, pool_w, pool_s,
                  w_out_even, w_in_odd, conv_w, conv_b, w_out_odd):
    o1, o2, o3, o4 = GLA_QK, 2 * GLA_QK, 2 * GLA_QK + MIX_A, 2 * GLA_QK + 2 * MIX_A
    o5 = o4 + 2 * GATE_RANK
    w_even = jnp.concatenate(
        [w_in_even[:, :, :o4], w_in_even[:, :, o5:], w_in_even[:, :, o4:o5],
         jnp.zeros((N_EVEN, D_MODEL, LANES - 2 * GATE_RANK), F32)], axis=-1).astype(BF16)
    wgate = jnp.zeros((N_EVEN, LANES, 2 * GLA_QK), F32)
    wgate = wgate.at[:, :GATE_RANK, :GLA_QK].set(w_a2[:, 0])
    wgate = wgate.at[:, GATE_RANK:2 * GATE_RANK, GLA_QK:].set(w_a2[:, 1])
    return dict(
        norm_g=norm_g.reshape(DEPTH, 2, 1, D_MODEL),
        final_g=final_g.reshape(1, D_MODEL),
        w1=w_mlp1.astype(BF16), w2=w_mlp2.astype(BF16),
        w_even=w_even, wgate=wgate.astype(BF16), bgate=b_a2.reshape(N_EVEN, 1, 2 * GLA_QK),
        gn=gla_norm_g.reshape(N_EVEN, 1, GLA_DV),
        pool_w=pool_w.astype(BF16), pool_s=pool_s.reshape(N_EVEN, 1, MIX_B),
        w_out_even=w_out_even.astype(BF16),
        w_odd=w_in_odd.astype(BF16), conv_w=conv_w, conv_b=conv_b.reshape(-1, 1, MIX_D),
        w_out_odd=w_out_odd.astype(BF16),
    )


def _trunk(x, mods, gla_init, shared_mod, want_state, wp):
    batch, seq_len, _ = x.shape
    x = x.reshape(batch * seq_len, D_MODEL)
    tables = _fourier_tables(seq_len)
    states = []
    for l in range(DEPTH):
        j = l // 2
        mod = mods[l]
        if l % 2 == 0:
            qk, v, gate, u, lf = _in_even(x, mod, wp['norm_g'][l, 0], wp['w_even'][j], wp['wgate'][j],
                                          wp['bgate'][j], seq_len, shared_mod)
            s0 = None if gla_init is None else gla_init[:, j].reshape(batch, 2, GLA_QK, GLA_DV)
            a, s = _gla(qk, v, lf, gate, wp['gn'][j], s0, batch, seq_len, want_state)
            if want_state:
                states.append(s.reshape(batch, 2, GLA_HEADS, GLA_DK, GLA_DV))
            b = _pool(u, wp['pool_w'][j], wp['pool_s'][j], batch, seq_len)
            w_out = wp['w_out_even'][j]
        else:
            f, z, bg = _in_odd(x, mod, wp['norm_g'][l, 0], wp['w_odd'][j], seq_len, shared_mod)
            a = _fourier(f, tables, batch, seq_len)
            b = _conv(z, bg, wp['conv_w'][j], wp['conv_b'][j], batch, seq_len)
            w_out = wp['w_out_odd'][j]
        x = _post(x, a, b, mod, wp['norm_g'][l, 1], w_out, wp['w1'][l], wp['w2'][l], wp['final_g'],
                  seq_len, shared_mod, final=(l == DEPTH - 1))
    y = x.reshape(batch, seq_len, D_MODEL)
    return y, (jnp.stack(states, axis=1) if want_state else None)


def kernel(x_prompt, x_sample, state_gla, c, c_ctx, norm_g, final_g, w_ada, b_ada, w_mlp1, w_mlp2,
           w_in_even, w_a2, b_a2, gla_norm_g, pool_w, pool_s, w_out_even,
           w_in_odd, conv_w, conv_b, w_out_odd):
    wp = _prep_weights(norm_g, final_g, w_mlp1, w_mlp2, w_in_even, w_a2, b_a2, gla_norm_g, pool_w, pool_s,
                       w_out_even, w_in_odd, conv_w, conv_b, w_out_odd)
    n_dec = c.shape[0]
    cond = jnp.concatenate([c_ctx[None, :], c, jnp.zeros((8 - 1 - n_dec, D_MODEL), F32)], axis=0)
    mod_all = _ada(cond, w_ada, b_ada)
    mods_ctx = [mod_all[l, 0:1].reshape(1, 1, 6 * D_MODEL) for l in range(DEPTH)]
    mods_lat = [mod_all[l, 1:1 + n_dec].reshape(n_dec, 1, 6 * D_MODEL) for l in range(DEPTH)]

    y_prompt, new_state = _trunk(x_prompt, mods_ctx, None, True, True, wp)
    x_lat = x_sample + _grid_pos_embed(x_sample.shape[1]).astype(x_sample.dtype)[None]
    y_sample, _ = _trunk(x_lat, mods_lat, state_gla, False, False, wp)
    return (y_prompt, y_sample, new_state.astype(x_prompt.dtype))
```

```python
import functools
import math

import jax
import jax.numpy as jnp
from jax import lax
from jax.experimental import pallas as pl
from jax.experimental.pallas import tpu as pltpu

D_MODEL = 1024
DEPTH = 4
N_EVEN = 2
GRID_W = 64
MIX_A = 512
MIX_B = 512
GLA_HEADS = 4
GLA_DV = 128
GLA_DK = 64
GLA_QK = 256
LOG2_DK = 6
LOG2_DV = 7
GATE_RANK = 16
GATE_TAU = 16.0
POOL_WINDOWS = (2, 4, 8, 16)
POOL_GW = 128
MIX_C = 512
MIX_D = 512
FOURIER_GW = 128
D_FF = 4096
EPS = 1e-6

LANES = 128
SUBLANES = 8
TOKEN_TILE = 512
GLA_CHUNK = 256
EVEN_IN_PAD = 2048 + LANES
VMEM_LIMIT = 56 * 1024 * 1024

F32 = jnp.float32
BF16 = jnp.bfloat16


def _dot(a, b):
    return jnp.dot(a, b, preferred_element_type=F32)


def _dot_nt(a, b):
    return lax.dot_general(a, b, (((1,), (1,)), ((), ())), preferred_element_type=F32)


def _split_dot(m, x):
    hi = x.astype(BF16)
    lo = (x - hi.astype(F32)).astype(BF16)
    return _dot(m, hi) + _dot(m, lo)


def _silu(x):
    return x * (1.0 / (1.0 + jnp.exp(-x)))


def _rmsnorm(x, g):
    ms = jnp.mean(x * x, axis=-1, keepdims=True)
    return (x * lax.rsqrt(ms + EPS)) * g


def _params(n_axes):
    return pltpu.CompilerParams(dimension_semantics=("arbitrary",) * n_axes, vmem_limit_bytes=VMEM_LIMIT)


def _resident(shape, layer=None):
    zeros = (0,) * len(shape)
    if layer is None:
        return pl.BlockSpec(shape, lambda *_: zeros, pipeline_mode=pl.Buffered(1))
    return pl.BlockSpec((None,) + tuple(shape), lambda *_: (layer,) + zeros, pipeline_mode=pl.Buffered(1))


ADA_COLS = 1536
COND_ROWS = SUBLANES


def _ada_kernel(cond_ref, w_ref, b_ref, out_ref):
    s = _silu(cond_ref[...]).astype(BF16)
    out_ref[0] = _dot(s, w_ref[0].astype(BF16)) + b_ref[0]


def _ada(cond, w_ada, b_ada):
    n_out = w_ada.shape[-1]
    return pl.pallas_call(
        _ada_kernel,
        grid=(DEPTH, n_out // ADA_COLS),
        in_specs=[
            pl.BlockSpec((COND_ROWS, D_MODEL), lambda l, j: (0, 0)),
            pl.BlockSpec((1, D_MODEL, ADA_COLS), lambda l, j: (l, 0, j)),
            pl.BlockSpec((1, 1, ADA_COLS), lambda l, j: (l, 0, j)),
        ],
        out_specs=pl.BlockSpec((1, COND_ROWS, ADA_COLS), lambda l, j: (l, 0, j)),
        out_shape=jax.ShapeDtypeStruct((DEPTH, COND_ROWS, n_out), F32),
        compiler_params=_params(2),
        name="ada_mod",
    )(cond, w_ada, b_ada.reshape(DEPTH, 1, n_out))


def _modulated_norm(x, mod_ref, g_ref, which):
    shift = mod_ref[0, :, (3 * which) * D_MODEL:(3 * which + 1) * D_MODEL]
    scale = mod_ref[0, :, (3 * which + 1) * D_MODEL:(3 * which + 2) * D_MODEL]
    return _rmsnorm(x, g_ref[...]) * (1.0 + scale) + shift


def _load_x(x_ref, pos_ref):
    return x_ref[...] if pos_ref is None else x_ref[...] + pos_ref[...]


def _in_even_kernel(*refs, has_pos):
    x_ref, refs = refs[0], refs[1:]
    pos_ref = None
    if has_pos:
        pos_ref, refs = refs[0], refs[1:]
    mod_ref, g_ref, w_ref, wgate_ref, bgate_ref, qk_ref, v_ref, gate_ref, u_ref, lf_ref = refs
    h = _modulated_norm(_load_x(x_ref, pos_ref), mod_ref, g_ref, 0).astype(BF16)
    p = _dot(h, w_ref[...])
    qk_ref[:, :GLA_QK] = p[:, :GLA_QK] * (GLA_DK ** -0.5)
    qk_ref[:, GLA_QK:] = p[:, GLA_QK:2 * GLA_QK]
    v_ref[...] = p[:, 512:1024].astype(BF16)
    gate_ref[...] = p[:, 1024:1536]
    u_ref[...] = p[:, 1536:2048]
    pre = _dot(p[:, 2048:].astype(BF16), wgate_ref[...]) + bgate_ref[...]
    log_sig = jnp.minimum(pre, 0.0) - jnp.log(1.0 + jnp.exp(-jnp.abs(pre)))
    lf_ref[...] = log_sig / GATE_TAU


def _in_odd_kernel(x_ref, mod_ref, g_ref, w_ref, f_ref, z_ref, bg_ref):
    h = _modulated_norm(x_ref[...], mod_ref, g_ref, 0).astype(BF16)
    p = _dot(h, w_ref[...])
    f_ref[...] = p[:, :MIX_C].astype(BF16)
    z_ref[...] = p[:, MIX_C + 2 * MIX_D:] * p[:, MIX_C:MIX_C + MIX_D]
    bg_ref[...] = p[:, MIX_C + MIX_D:MIX_C + 2 * MIX_D]


def _mod_spec(layer, seq_len, shared):
    base = layer * COND_ROWS
    if shared:
        return pl.BlockSpec((1, 1, 6 * D_MODEL), lambda i: (base, 0, 0))
    return pl.BlockSpec((1, 1, 6 * D_MODEL), lambda i: (base + 1 + i * TOKEN_TILE // seq_len, 0, 0))


def _tile_spec(cols):
    return pl.BlockSpec((TOKEN_TILE, cols), lambda i: (i, 0))


def _pos_spec(seq_len):
    tiles = seq_len // TOKEN_TILE
    return pl.BlockSpec((TOKEN_TILE, D_MODEL), lambda i: (i % tiles, 0))


def _in_even(x, pos, mods, wp, layer, seq_len, shared):
    n = x.shape[0]
    j = layer // 2
    outs = [(512, F32), (512, BF16), (512, F32), (512, F32), (512, F32)]
    has_pos = pos is not None
    in_specs = [_tile_spec(D_MODEL)] + ([_pos_spec(seq_len)] if has_pos else []) + [
        _mod_spec(layer, seq_len, shared), _resident((1, D_MODEL), 2 * layer),
        _resident((D_MODEL, EVEN_IN_PAD), j), _resident((LANES, 2 * GLA_QK), j), _resident((1, 2 * GLA_QK), j),
    ]
    args = [x] + ([pos] if has_pos else []) + [mods, wp['norm_g'], wp['w_even'], wp['wgate'], wp['bgate']]
    return pl.pallas_call(
        functools.partial(_in_even_kernel, has_pos=has_pos),
        grid=(n // TOKEN_TILE,),
        in_specs=in_specs,
        out_specs=[_tile_spec(c) for c, _ in outs],
        out_shape=[jax.ShapeDtypeStruct((n, c), dt) for c, dt in outs],
        compiler_params=_params(1),
        name="in_even",
    )(*args)


def _in_odd(x, mods, wp, layer, seq_len, shared):
    n = x.shape[0]
    outs = [(MIX_C, BF16), (MIX_D, F32), (MIX_D, F32)]
    return pl.pallas_call(
        _in_odd_kernel,
        grid=(n // TOKEN_TILE,),
        in_specs=[
            _tile_spec(D_MODEL), _mod_spec(layer, seq_len, shared), _resident((1, D_MODEL), 2 * layer),
            _resident((D_MODEL, MIX_C + 3 * MIX_D), layer // 2),
        ],
        out_specs=[_tile_spec(c) for c, _ in outs],
        out_shape=[jax.ShapeDtypeStruct((n, c), dt) for c, dt in outs],
        compiler_params=_params(1),
        name="in_odd",
    )(x, mods, wp['norm_g'], wp['w_odd'])


FF_BLOCK = 1024


def _post_kernel(*refs, has_pos, final):
    x_ref, refs = refs[0], refs[1:]
    pos_ref = None
    if has_pos:
        pos_ref, refs = refs[0], refs[1:]
    a_ref, b_ref, mod_ref, g_ref, wo_ref, w1_ref, w2_ref, fg_ref, out_ref = refs
    half = D_MODEL // 2
    y = _dot(a_ref[...], wo_ref[:half, :]) + _dot(b_ref[...], wo_ref[half:, :])
    gate1 = mod_ref[0, :, 2 * D_MODEL:3 * D_MODEL]
    gate2 = mod_ref[0, :, 5 * D_MODEL:6 * D_MODEL]
    x1 = _load_x(x_ref, pos_ref) + gate1 * y
    h = _modulated_norm(x1, mod_ref, g_ref, 1).astype(BF16)
    acc = jnp.zeros_like(x1)
    for c in range(D_FF // FF_BLOCK):
        a = jnp.maximum(_dot(h, w1_ref[:, c * FF_BLOCK:(c + 1) * FF_BLOCK]), 0.0)
        acc = acc + _dot((a * a).astype(BF16), w2_ref[c * FF_BLOCK:(c + 1) * FF_BLOCK, :])
    x2 = x1 + gate2 * acc
    if final:
        x2 = _rmsnorm(x2, fg_ref[...])
    out_ref[...] = x2


def _post(x, pos, a, b, mods, wp, layer, seq_len, shared):
    n = x.shape[0]
    j = layer // 2
    final = layer == DEPTH - 1
    has_pos = pos is not None
    w_out = wp['w_out_even'] if layer % 2 == 0 else wp['w_out_odd']
    in_specs = [_tile_spec(D_MODEL)] + ([_pos_spec(seq_len)] if has_pos else []) + [
        _tile_spec(D_MODEL // 2), _tile_spec(D_MODEL // 2),
        _mod_spec(layer, seq_len, shared), _resident((1, D_MODEL), 2 * layer + 1),
        _resident((D_MODEL, D_MODEL), j), _resident((D_MODEL, D_FF), layer), _resident((D_FF, D_MODEL), layer),
        _resident((1, D_MODEL)),
    ]
    args = [x] + ([pos] if has_pos else []) + [a, b, mods, wp['norm_g'], w_out, wp['w1'], wp['w2'], wp['final_g']]
    return pl.pallas_call(
        functools.partial(_post_kernel, has_pos=has_pos, final=final),
        grid=(n // TOKEN_TILE,),
        in_specs=in_specs,
        out_specs=_tile_spec(D_MODEL),
        out_shape=jax.ShapeDtypeStruct((n, D_MODEL), F32),
        compiler_params=_params(1),
        name="post_final" if final else "post",
    )(*args)


def _head_block_mask():
    r = lax.broadcasted_iota(jnp.int32, (GLA_QK, MIX_A), 0) >> LOG2_DK
    c = lax.broadcasted_iota(jnp.int32, (GLA_QK, MIX_A), 1) >> LOG2_DV
    return jnp.where(r == c, 1.0, 0.0).astype(F32)


def _gla_kernel(*refs, n_chunks, has_state, want_state):
    qk_ref, v_ref, lf_ref, gate_ref, gn_ref = refs[:5]
    refs = refs[5:]
    s0_ref = None
    if has_state:
        s0_ref, refs = refs[0], refs[1:]
    o_ref, refs = refs[0], refs[1:]
    sout_ref = None
    if want_state:
        sout_ref, refs = refs[0], refs[1:]
    state_ref, fwd_ref = refs

    C = GLA_CHUNK
    direction = pl.program_id(1)
    step = pl.program_id(2)
    backward = direction == 1

    @pl.when(step == 0)
    def _():
        if has_state:
            s0 = s0_ref[0, 0]
            state_ref[...] = jnp.concatenate([s0] * GLA_HEADS, axis=1) * _head_block_mask()
        else:
            state_ref[...] = jnp.zeros_like(state_ref)

    q = qk_ref[:, :GLA_QK]
    k = qk_ref[:, GLA_QK:]
    v = v_ref[...]
    lf = jnp.where(backward, lf_ref[:, GLA_QK:], lf_ref[:, :GLA_QK])

    row = lax.broadcasted_iota(jnp.int32, (C, C), 0)
    col = lax.broadcasted_iota(jnp.int32, (C, C), 1)
    sign = 1 - 2 * direction
    causal = (row - col) * sign >= 0
    cum = _split_dot(jnp.where(causal, 1.0, 0.0).astype(BF16), lf)
    total_row = jnp.where(backward, cum[0:1, :], cum[C - 1:C, :])
    mid = cum[C // 2:C // 2 + 1, :]

    state = state_ref[...]
    q_in = (q * jnp.exp(cum)).astype(BF16)
    o = _dot(q_in, state.astype(BF16))

    q_c = (q * jnp.exp(cum - mid)).astype(BF16)
    k_c = k * jnp.exp(mid - cum)
    lane_head = lax.broadcasted_iota(jnp.int32, (C, GLA_QK), 1) >> LOG2_DK
    intra = []
    for h in range(GLA_HEADS):
        k_h = jnp.where(lane_head == h, k_c, 0.0).astype(BF16)
        att = jnp.where(causal, _dot_nt(q_c, k_h), 0.0).astype(BF16)
        intra.append(_dot(att, v[:, h * GLA_DV:(h + 1) * GLA_DV]))
    o = o + jnp.concatenate(intra, axis=1)

    k_out = (k * jnp.exp(total_row - cum)).T.astype(BF16)
    decay = jnp.exp(jnp.sum(lf.T, axis=1, keepdims=True))
    mask = _head_block_mask()
    new_state = state * decay + _dot(k_out, v) * mask
    state_ref[...] = new_state

    chunk = pl.multiple_of(jnp.where(backward, n_chunks - 1 - step, step) * C, C)

    @pl.when(direction == 0)
    def _():
        fwd_ref[pl.ds(chunk, C), :] = o

    @pl.when(backward)
    def _():
        both = fwd_ref[pl.ds(chunk, C), :] + o
        normed = []
        for h in range(GLA_HEADS):
            o_h = both[:, h * GLA_DV:(h + 1) * GLA_DV]
            ms = jnp.mean(o_h * o_h, axis=-1, keepdims=True)
            normed.append(o_h * lax.rsqrt(ms + EPS) * gn_ref[...])
        o_ref[...] = (jnp.concatenate(normed, axis=1) * _silu(gate_ref[...])).astype(BF16)

    if want_state:
        @pl.when(step == n_chunks - 1)
        def _():
            s = new_state[:, :GLA_DV]
            for h in range(1, GLA_HEADS):
                s = s + new_state[:, h * GLA_DV:(h + 1) * GLA_DV]
            sout_ref[0, 0] = s


def _gla(qk, v, lf, gate, wp, layer, s0, batch, seq_len, want_state):
    n = qk.shape[0]
    n_chunks = seq_len // GLA_CHUNK
    has_state = s0 is not None

    def chunk_map(b, d, c):
        return (b * n_chunks + jnp.where(d == 0, c, n_chunks - 1 - c), 0)

    def out_map(b, d, c):
        return (b * n_chunks + jnp.where(d == 0, n_chunks - 1, n_chunks - 1 - c), 0)

    def state_map(b, d, c):
        return (b, d, 0, 0)

    in_specs = [
        pl.BlockSpec((GLA_CHUNK, 2 * GLA_QK), chunk_map),
        pl.BlockSpec((GLA_CHUNK, MIX_A), chunk_map),
        pl.BlockSpec((GLA_CHUNK, 2 * GLA_QK), chunk_map),
        pl.BlockSpec((GLA_CHUNK, MIX_A), chunk_map),
        _resident((1, GLA_DV), layer // 2),
    ]
    args = [qk, v, lf, gate, wp['gn']]
    if has_state:
        in_specs.append(pl.BlockSpec((1, 1, GLA_QK, GLA_DV), state_map))
        args.append(s0)
    out_specs = [pl.BlockSpec((GLA_CHUNK, MIX_A), out_map)]
    out_shape = [jax.ShapeDtypeStruct((n, MIX_A), BF16)]
    if want_state:
        out_specs.append(pl.BlockSpec((1, 1, GLA_QK, GLA_DV), state_map))
        out_shape.append(jax.ShapeDtypeStruct((batch, 2, GLA_QK, GLA_DV), F32))
    outs = pl.pallas_call(
        functools.partial(_gla_kernel, n_chunks=n_chunks, has_state=has_state, want_state=want_state),
        grid=(batch, 2, n_chunks),
        in_specs=in_specs,
        out_specs=out_specs,
        out_shape=out_shape,
        scratch_shapes=[pltpu.VMEM((GLA_QK, MIX_A), F32), pltpu.VMEM((seq_len, MIX_A), F32)],
        compiler_params=_params(3),
        name="gla",
    )(*args)
    return (outs[0], outs[1]) if want_state else (outs[0], None)


def _shift_down(x, s):
    t = lax.broadcasted_iota(jnp.int32, x.shape, 0)
    return jnp.where(t >= s, pltpu.roll(x, s, axis=0), 0.0)


def _shift_up(x, s):
    n = x.shape[0]
    t = lax.broadcasted_iota(jnp.int32, x.shape, 0)
    return jnp.where(t < n - s, pltpu.roll(x, n - s, axis=0), 0.0)


def _pool_kernel(u_ref, w_ref, s_ref, out_ref):
    n = u_ref.shape[0]
    t = lax.broadcasted_iota(jnp.int32, (n, POOL_GW), 0)
    for gi, w in enumerate(POOL_WINDOWS):
        sl = slice(gi * POOL_GW, (gi + 1) * POOL_GW)
        u = u_ref[:, sl]
        half = w // 2
        trail, lead, m = u, u, 1
        while m < half:
            trail = trail + _shift_down(trail, m)
            lead = lead + _shift_up(lead, m)
            m *= 2
        window = _shift_down(trail, 1) + lead
        cnt = (jnp.minimum(t + (w - half), n) - jnp.maximum(t - half, 0)).astype(F32)
        pooled = window / cnt - u
        y = _dot(pooled.astype(BF16), w_ref[gi])
        out_ref[:, sl] = (y * s_ref[:, sl]).astype(BF16)


def _pool(u, wp, layer, batch, seq_len):
    n = u.shape[0]
    j = layer // 2
    return pl.pallas_call(
        _pool_kernel,
        grid=(batch,),
        in_specs=[
            pl.BlockSpec((seq_len, MIX_B), lambda b: (b, 0)),
            _resident((len(POOL_WINDOWS), POOL_GW, POOL_GW), j),
            _resident((1, MIX_B), j),
        ],
        out_specs=pl.BlockSpec((seq_len, MIX_B), lambda b: (b, 0)),
        out_shape=jax.ShapeDtypeStruct((n, MIX_B), BF16),
        compiler_params=_params(1),
        name="pool",
    )(u, wp['pool_w'], wp['pool_s'])


def _conv_kernel(z_ref, bg_ref, w_ref, b_ref, out_ref):
    z = z_ref[...]
    conv = b_ref[...] + _shift_down(z, 1) * w_ref[0:1, :]
    conv = conv + z * w_ref[1:2, :]
    conv = conv + _shift_up(z, 1) * w_ref[2:3, :]
    out_ref[...] = (bg_ref[...] * conv).astype(BF16)


def _conv(z, bg, wp, layer, batch, seq_len):
    n = z.shape[0]
    j = layer // 2
    seq_spec = pl.BlockSpec((seq_len, MIX_D), lambda b: (b, 0))
    return pl.pallas_call(
        _conv_kernel,
        grid=(batch,),
        in_specs=[seq_spec, seq_spec, _resident(wp['conv_w'].shape[1:], j), _resident((1, MIX_D), j)],
        out_specs=seq_spec,
        out_shape=jax.ShapeDtypeStruct((n, MIX_D), BF16),
        compiler_params=_params(1),
        name="gated_conv",
    )(z, bg, wp['conv_w'], wp['conv_b'])


DFT_ROWS = 256
DFT_LOW = 32


def _dft_rows(rows, n):
    k = jnp.arange(n, dtype=jnp.int32)
    ang = ((rows[:, None] * k[None, :]) % n).astype(F32) * (2.0 * math.pi / n)
    return jnp.cos(ang), jnp.sin(ang)


def _dft_tables(n):
    c_lo, s_lo = _dft_rows(jnp.arange(DFT_LOW, dtype=jnp.int32), n)
    c_hi, s_hi = _dft_rows(jnp.arange(n // DFT_LOW, dtype=jnp.int32) * DFT_LOW, n)
    cos = c_hi[:, None, :] * c_lo[None, :, :] - s_hi[:, None, :] * s_lo[None, :, :]
    sin = s_hi[:, None, :] * c_lo[None, :, :] + c_hi[:, None, :] * s_lo[None, :, :]
    return cos.reshape(n, n), sin.reshape(n, n)


def _fourier_kernel(f_ref, cs_ref, cl_ref, sl_ref, out_ref, xc_ref, xs_ref, *, scale):
    n = f_ref.shape[0]
    for g in range(MIX_C // FOURIER_GW):
        sl = slice(g * FOURIER_GW, (g + 1) * FOURIER_GW)
        x = _dot(f_ref[:, sl], cs_ref[...])
        xc_ref[:, sl] = x[:, :FOURIER_GW].astype(BF16)
        xs_ref[:, sl] = x[:, FOURIER_GW:].astype(BF16)

    def body(r, carry):
        rows = pl.ds(pl.multiple_of(r * DFT_ROWS, DFT_ROWS), DFT_ROWS)
        re = _dot(cl_ref[rows, :], xc_ref[...]) - _dot(sl_ref[rows, :], xs_ref[...])
        out_ref[rows, :] = (re * scale).astype(BF16)
        return carry

    lax.fori_loop(0, n // DFT_ROWS, body, 0)


def _fourier_tables(seq_len):
    cc, sc = _dft_rows(jnp.arange(FOURIER_GW, dtype=jnp.int32), FOURIER_GW)
    cl, sl = _dft_tables(seq_len)
    return jnp.concatenate([cc, sc], axis=1).astype(BF16), cl.astype(BF16), sl.astype(BF16)


def _fourier(f, tables, batch, seq_len):
    n = f.shape[0]
    scale = 1.0 / math.sqrt(seq_len * FOURIER_GW)
    seq_spec = pl.BlockSpec((seq_len, MIX_C), lambda b: (b, 0))
    return pl.pallas_call(
        functools.partial(_fourier_kernel, scale=scale),
        grid=(batch,),
        in_specs=[seq_spec, _resident((FOURIER_GW, 2 * FOURIER_GW)),
                  _resident((seq_len, seq_len)), _resident((seq_len, seq_len))],
        out_specs=seq_spec,
        out_shape=jax.ShapeDtypeStruct((n, MIX_C), BF16),
        scratch_shapes=[pltpu.VMEM((seq_len, MIX_C), BF16), pltpu.VMEM((seq_len, MIX_C), BF16)],
        compiler_params=_params(1),
        name="fourier",
    )(f, *tables)


def _grid_pos_embed(n_tok):
    rows = n_tok // GRID_W
    quarter = D_MODEL // 4
    freqs = 1.0 / (10000.0 ** (jnp.arange(quarter, dtype=F32) / quarter))
    ar = jnp.arange(rows, dtype=F32)[:, None] * freqs
    ac = jnp.arange(GRID_W, dtype=F32)[:, None] * freqs

    def per_row(t):
        return jnp.broadcast_to(t[:, None, :], (rows, GRID_W, quarter))

    def per_col(t):
        return jnp.broadcast_to(t[None, :, :], (rows, GRID_W, quarter))

    pos = jnp.concatenate([per_row(jnp.sin(ar)), per_row(jnp.cos(ar)), per_col(jnp.sin(ac)), per_col(jnp.cos(ac))],
                          axis=-1)
    return pos.reshape(n_tok, D_MODEL)


def _prep_weights(norm_g, final_g, w_mlp1, w_mlp2, w_in_even, w_a2, b_a2, gla_norm_g, pool_w, pool_s,
                  w_out_even, w_in_odd, conv_w, conv_b, w_out_odd):
    o4 = 2 * GLA_QK + 2 * MIX_A
    o5 = o4 + 2 * GATE_RANK
    w_even = jnp.concatenate(
        [w_in_even[:, :, :o4], w_in_even[:, :, o5:], w_in_even[:, :, o4:o5],
         jnp.zeros((N_EVEN, D_MODEL, LANES - 2 * GATE_RANK), F32)], axis=-1).astype(BF16)
    wgate = jnp.zeros((N_EVEN, LANES, 2 * GLA_QK), F32)
    wgate = wgate.at[:, :GATE_RANK, :GLA_QK].set(w_a2[:, 0])
    wgate = wgate.at[:, GATE_RANK:2 * GATE_RANK, GLA_QK:].set(w_a2[:, 1])
    return dict(
        norm_g=norm_g.reshape(DEPTH * 2, 1, D_MODEL),
        final_g=final_g.reshape(1, D_MODEL),
        w1=w_mlp1.astype(BF16), w2=w_mlp2.astype(BF16),
        w_even=w_even, wgate=wgate.astype(BF16), bgate=b_a2.reshape(N_EVEN, 1, 2 * GLA_QK),
        gn=gla_norm_g.reshape(N_EVEN, 1, GLA_DV),
        pool_w=pool_w.astype(BF16), pool_s=pool_s.reshape(N_EVEN, 1, MIX_B),
        w_out_even=w_out_even.astype(BF16),
        w_odd=w_in_odd.astype(BF16), conv_w=conv_w, conv_b=conv_b.reshape(-1, 1, MIX_D),
        w_out_odd=w_out_odd.astype(BF16),
    )


def _trunk(x, pos, mods, gla_init, shared_mod, want_state, wp):
    batch, seq_len, _ = x.shape
    x = x.reshape(batch * seq_len, D_MODEL)
    tables = _fourier_tables(seq_len)
    states = []
    for l in range(DEPTH):
        j = l // 2
        if l % 2 == 0:
            qk, v, gate, u, lf = _in_even(x, pos, mods, wp, l, seq_len, shared_mod)
            s0 = None if gla_init is None else gla_init[:, j].reshape(batch, 2, GLA_QK, GLA_DV)
            a, s = _gla(qk, v, lf, gate, wp, l, s0, batch, seq_len, want_state)
            if want_state:
                states.append(s.reshape(batch, 2, GLA_HEADS, GLA_DK, GLA_DV))
            b = _pool(u, wp, l, batch, seq_len)
        else:
            f, z, bg = _in_odd(x, mods, wp, l, seq_len, shared_mod)
            a = _fourier(f, tables, batch, seq_len)
            b = _conv(z, bg, wp, l, batch, seq_len)
        x = _post(x, pos, a, b, mods, wp, l, seq_len, shared_mod)
        pos = None
    y = x.reshape(batch, seq_len, D_MODEL)
    return y, (jnp.stack(states, axis=1) if want_state else None)


def kernel(x_prompt, x_sample, state_gla, c, c_ctx, norm_g, final_g, w_ada, b_ada, w_mlp1, w_mlp2,
           w_in_even, w_a2, b_a2, gla_norm_g, pool_w, pool_s, w_out_even,
           w_in_odd, conv_w, conv_b, w_out_odd):
    wp = _prep_weights(norm_g, final_g, w_mlp1, w_mlp2, w_in_even, w_a2, b_a2, gla_norm_g, pool_w, pool_s,
                       w_out_even, w_in_odd, conv_w, conv_b, w_out_odd)
    n_dec = c.shape[0]
    cond = jnp.concatenate([c_ctx[None, :], c, jnp.zeros((COND_ROWS - 1 - n_dec, D_MODEL), F32)], axis=0)
    mods = _ada(cond, w_ada, b_ada).reshape(DEPTH * COND_ROWS, 1, 6 * D_MODEL)

    y_prompt, new_state = _trunk(x_prompt, None, mods, None, True, True, wp)
    pos = _grid_pos_embed(x_sample.shape[1]).astype(x_sample.dtype)
    y_sample, _ = _trunk(x_sample, pos, mods, state_gla, False, False, wp)
    return (y_prompt, y_sample, new_state.astype(x_prompt.dtype))
```

```python
import functools
import math

import jax
import jax.numpy as jnp
from jax import lax
from jax.experimental import pallas as pl
from jax.experimental.pallas import tpu as pltpu

D_MODEL = 1024
DEPTH = 4
N_EVEN = 2
GRID_W = 64
MIX_A = 512
MIX_B = 512
GLA_HEADS = 4
GLA_DV = 128
GLA_DK = 64
GLA_QK = 256
LOG2_DK = 6
LOG2_DV = 7
GATE_RANK = 16
GATE_TAU = 16.0
POOL_WINDOWS = (2, 4, 8, 16)
POOL_GW = 128
MIX_C = 512
MIX_D = 512
FOURIER_GW = 128
D_FF = 4096
EPS = 1e-6

LANES = 128
SUBLANES = 8
TOKEN_TILE = 512
GLA_CHUNK = 256
EVEN_IN_PAD = 2048 + LANES
VMEM_LIMIT = 56 * 1024 * 1024

F32 = jnp.float32
BF16 = jnp.bfloat16


def _dot(a, b):
    return jnp.dot(a, b, preferred_element_type=F32)


def _dot_nt(a, b):
    return lax.dot_general(a, b, (((1,), (1,)), ((), ())), preferred_element_type=F32)


def _split_dot(m, x):
    hi = x.astype(BF16)
    lo = (x - hi.astype(F32)).astype(BF16)
    return _dot(m, hi) + _dot(m, lo)


def _silu(x):
    return x * (1.0 / (1.0 + jnp.exp(-x)))


def _rmsnorm(x, g):
    ms = jnp.mean(x * x, axis=-1, keepdims=True)
    return (x * lax.rsqrt(ms + EPS)) * g


def _params(n_axes):
    return pltpu.CompilerParams(dimension_semantics=("arbitrary",) * n_axes, vmem_limit_bytes=VMEM_LIMIT)


def _resident(shape, layer=None):
    zeros = (0,) * len(shape)
    if layer is None:
        return pl.BlockSpec(shape, lambda *_: zeros, pipeline_mode=pl.Buffered(1))
    return pl.BlockSpec((None,) + tuple(shape), lambda *_: (layer,) + zeros, pipeline_mode=pl.Buffered(1))


ADA_COLS = 1536
COND_ROWS = SUBLANES


def _ada_kernel(cond_ref, w_ref, b_ref, out_ref):
    s = _silu(cond_ref[...]).astype(BF16)
    out_ref[0] = _dot(s, w_ref[0].astype(BF16)) + b_ref[0]


def _ada(cond, w_ada, b_ada):
    n_out = w_ada.shape[-1]
    return pl.pallas_call(
        _ada_kernel,
        grid=(DEPTH, n_out // ADA_COLS),
        in_specs=[
            pl.BlockSpec((COND_ROWS, D_MODEL), lambda l, j: (0, 0)),
            pl.BlockSpec((1, D_MODEL, ADA_COLS), lambda l, j: (l, 0, j)),
            pl.BlockSpec((1, 1, ADA_COLS), lambda l, j: (l, 0, j)),
        ],
        out_specs=pl.BlockSpec((1, COND_ROWS, ADA_COLS), lambda l, j: (l, 0, j)),
        out_shape=jax.ShapeDtypeStruct((DEPTH, COND_ROWS, n_out), F32),
        compiler_params=_params(2),
        name="ada_mod",
    )(cond, w_ada, b_ada.reshape(DEPTH, 1, n_out))


def _modulated_norm(x, mod_ref, g_ref, which):
    shift = mod_ref[0, :, (3 * which) * D_MODEL:(3 * which + 1) * D_MODEL]
    scale = mod_ref[0, :, (3 * which + 1) * D_MODEL:(3 * which + 2) * D_MODEL]
    return _rmsnorm(x, g_ref[...]) * (1.0 + scale) + shift


def _load_x(x_ref, pos_ref):
    return x_ref[...] if pos_ref is None else x_ref[...] + pos_ref[...]


HALO = SUBLANES
HALO_TILE = TOKEN_TILE + 2 * HALO
MAIN = slice(HALO, HALO + TOKEN_TILE)


def _project_with_halo(refs, has_pos, seq_len, pos_cols):
    x_refs, refs = refs[:3], refs[3:]
    x = jnp.concatenate([r[...] for r in x_refs], axis=0)
    if has_pos:
        pos_refs, refs = refs[:3], refs[3:]
        x = x + jnp.concatenate([r[...] for r in pos_refs], axis=0)
    mod_ref, g_ref, w_ref, refs = refs[0], refs[1], refs[2], refs[3:]
    h = _modulated_norm(x, mod_ref, g_ref, 0).astype(BF16)
    p = _dot(h, w_ref[...])
    row = lax.broadcasted_iota(jnp.int32, (HALO_TILE, pos_cols), 0)
    seq_pos = (pl.program_id(0) * TOKEN_TILE - HALO + row) & (seq_len - 1)
    return refs, p, seq_pos


def _shift_down(x, s, seq_pos):
    return jnp.where(seq_pos >= s, pltpu.roll(x, s, axis=0), 0.0)


def _shift_up(x, s, seq_pos, seq_len):
    return jnp.where(seq_pos < seq_len - s, pltpu.roll(x, x.shape[0] - s, axis=0), 0.0)


def _in_even_kernel(*refs, has_pos, seq_len):
    refs, p, seq_pos = _project_with_halo(refs, has_pos, seq_len, POOL_GW)
    wgate_ref, bgate_ref, pw_ref, ps_ref, qk_ref, v_ref, gate_ref, lf_ref, pool_ref = refs
    qk_ref[:, :GLA_QK] = p[MAIN, :GLA_QK] * (GLA_DK ** -0.5)
    qk_ref[:, GLA_QK:] = p[MAIN, GLA_QK:2 * GLA_QK]
    v_ref[...] = p[MAIN, 512:1024].astype(BF16)
    gate_ref[...] = p[MAIN, 1024:1536]
    pre = _dot(p[MAIN, 2048:].astype(BF16), wgate_ref[...]) + bgate_ref[...]
    log_sig = jnp.minimum(pre, 0.0) - jnp.log(1.0 + jnp.exp(-jnp.abs(pre)))
    lf_ref[...] = log_sig / GATE_TAU

    for gi, w in enumerate(POOL_WINDOWS):
        sl = slice(gi * POOL_GW, (gi + 1) * POOL_GW)
        u = p[:, 1536 + gi * POOL_GW:1536 + (gi + 1) * POOL_GW]
        half = w // 2
        trail, lead, m = u, u, 1
        while m < half:
            trail = trail + _shift_down(trail, m, seq_pos)
            lead = lead + _shift_up(lead, m, seq_pos, seq_len)
            m *= 2
        window = _shift_down(trail, 1, seq_pos) + lead
        cnt = (jnp.minimum(seq_pos + (w - half), seq_len) - jnp.maximum(seq_pos - half, 0)).astype(F32)
        pooled = (window / cnt - u)[MAIN]
        y = _dot(pooled.astype(BF16), pw_ref[gi])
        pool_ref[:, sl] = (y * ps_ref[:, sl]).astype(BF16)


def _in_odd_kernel(*refs, seq_len):
    refs, p, seq_pos = _project_with_halo(refs, False, seq_len, MIX_D)
    cw_ref, cb_ref, f_ref, conv_ref = refs
    f_ref[...] = p[MAIN, :MIX_C].astype(BF16)
    z = p[:, MIX_C + 2 * MIX_D:] * p[:, MIX_C:MIX_C + MIX_D]
    conv = cb_ref[...] + _shift_down(z, 1, seq_pos)[MAIN] * cw_ref[0:1, :]
    conv = conv + z[MAIN] * cw_ref[1:2, :]
    conv = conv + _shift_up(z, 1, seq_pos, seq_len)[MAIN] * cw_ref[2:3, :]
    conv_ref[...] = (p[MAIN, MIX_C + MIX_D:MIX_C + 2 * MIX_D] * conv).astype(BF16)


def _mod_spec(layer, seq_len, shared):
    base = layer * COND_ROWS
    if shared:
        return pl.BlockSpec((1, 1, 6 * D_MODEL), lambda i: (base, 0, 0))
    return pl.BlockSpec((1, 1, 6 * D_MODEL), lambda i: (base + 1 + i * TOKEN_TILE // seq_len, 0, 0))


def _tile_spec(cols):
    return pl.BlockSpec((TOKEN_TILE, cols), lambda i: (i, 0))


def _pos_spec(seq_len):
    tiles = seq_len // TOKEN_TILE
    return pl.BlockSpec((TOKEN_TILE, D_MODEL), lambda i: (i % tiles, 0))


def _halo_specs(n_rows, period_tiles):
    per_tile = TOKEN_TILE // HALO
    last = n_rows // HALO - 1
    return [
        pl.BlockSpec((HALO, D_MODEL), lambda i: (jnp.maximum((i % period_tiles) * per_tile - 1, 0), 0)),
        pl.BlockSpec((TOKEN_TILE, D_MODEL), lambda i: (i % period_tiles, 0)),
        pl.BlockSpec((HALO, D_MODEL), lambda i: (jnp.minimum((i % period_tiles + 1) * per_tile, last), 0)),
    ]


def _in_even(x, pos, mods, wp, layer, seq_len, shared):
    n = x.shape[0]
    j = layer // 2
    outs = [(512, F32), (512, BF16), (512, F32), (512, F32), (MIX_B, BF16)]
    has_pos = pos is not None
    in_specs = _halo_specs(n, n // TOKEN_TILE) + (_halo_specs(seq_len, seq_len // TOKEN_TILE) if has_pos else []) + [
        _mod_spec(layer, seq_len, shared), _resident((1, D_MODEL), 2 * layer),
        _resident((D_MODEL, EVEN_IN_PAD), j), _resident((LANES, 2 * GLA_QK), j), _resident((1, 2 * GLA_QK), j),
        _resident((len(POOL_WINDOWS), POOL_GW, POOL_GW), j), _resident((1, MIX_B), j),
    ]
    args = [x] * 3 + ([pos] * 3 if has_pos else []) + [
        mods, wp['norm_g'], wp['w_even'], wp['wgate'], wp['bgate'], wp['pool_w'], wp['pool_s']]
    return pl.pallas_call(
        functools.partial(_in_even_kernel, has_pos=has_pos, seq_len=seq_len),
        grid=(n // TOKEN_TILE,),
        in_specs=in_specs,
        out_specs=[_tile_spec(c) for c, _ in outs],
        out_shape=[jax.ShapeDtypeStruct((n, c), dt) for c, dt in outs],
        compiler_params=_params(1),
        name="in_even",
    )(*args)


def _in_odd(x, mods, wp, layer, seq_len, shared):
    n = x.shape[0]
    j = layer // 2
    outs = [(MIX_C, BF16), (MIX_D, BF16)]
    return pl.pallas_call(
        functools.partial(_in_odd_kernel, seq_len=seq_len),
        grid=(n // TOKEN_TILE,),
        in_specs=_halo_specs(n, n // TOKEN_TILE) + [
            _mod_spec(layer, seq_len, shared), _resident((1, D_MODEL), 2 * layer),
            _resident((D_MODEL, MIX_C + 3 * MIX_D), j),
            _resident(wp['conv_w'].shape[1:], j), _resident((1, MIX_D), j),
        ],
        out_specs=[_tile_spec(c) for c, _ in outs],
        out_shape=[jax.ShapeDtypeStruct((n, c), dt) for c, dt in outs],
        compiler_params=_params(1),
        name="in_odd",
    )(x, x, x, mods, wp['norm_g'], wp['w_odd'], wp['conv_w'], wp['conv_b'])


FF_BLOCK = 1024


def _post_kernel(*refs, has_pos, final):
    x_ref, refs = refs[0], refs[1:]
    pos_ref = None
    if has_pos:
        pos_ref, refs = refs[0], refs[1:]
    a_ref, b_ref, mod_ref, g_ref, wo_ref, w1_ref, w2_ref, fg_ref, out_ref = refs
    half = D_MODEL // 2
    y = _dot(a_ref[...], wo_ref[:half, :]) + _dot(b_ref[...], wo_ref[half:, :])
    gate1 = mod_ref[0, :, 2 * D_MODEL:3 * D_MODEL]
    gate2 = mod_ref[0, :, 5 * D_MODEL:6 * D_MODEL]
    x1 = _load_x(x_ref, pos_ref) + gate1 * y
    h = _modulated_norm(x1, mod_ref, g_ref, 1).astype(BF16)
    acc = jnp.zeros_like(x1)
    for c in range(D_FF // FF_BLOCK):
        a = jnp.maximum(_dot(h, w1_ref[:, c * FF_BLOCK:(c + 1) * FF_BLOCK]), 0.0)
        acc = acc + _dot((a * a).astype(BF16), w2_ref[c * FF_BLOCK:(c + 1) * FF_BLOCK, :])
    x2 = x1 + gate2 * acc
    if final:
        x2 = _rmsnorm(x2, fg_ref[...])
    out_ref[...] = x2


def _post(x, pos, a, b, mods, wp, layer, seq_len, shared):
    n = x.shape[0]
    j = layer // 2
    final = layer == DEPTH - 1
    has_pos = pos is not None
    w_out = wp['w_out_even'] if layer % 2 == 0 else wp['w_out_odd']
    in_specs = [_tile_spec(D_MODEL)] + ([_pos_spec(seq_len)] if has_pos else []) + [
        _tile_spec(D_MODEL // 2), _tile_spec(D_MODEL // 2),
        _mod_spec(layer, seq_len, shared), _resident((1, D_MODEL), 2 * layer + 1),
        _resident((D_MODEL, D_MODEL), j), _resident((D_MODEL, D_FF), layer), _resident((D_FF, D_MODEL), layer),
        _resident((1, D_MODEL)),
    ]
    args = [x] + ([pos] if has_pos else []) + [a, b, mods, wp['norm_g'], w_out, wp['w1'], wp['w2'], wp['final_g']]
    return pl.pallas_call(
        functools.partial(_post_kernel, has_pos=has_pos, final=final),
        grid=(n // TOKEN_TILE,),
        in_specs=in_specs,
        out_specs=_tile_spec(D_MODEL),
        out_shape=jax.ShapeDtypeStruct((n, D_MODEL), F32),
        compiler_params=_params(1),
        name="post_final" if final else "post",
    )(*args)


def _head_block_mask():
    r = lax.broadcasted_iota(jnp.int32, (GLA_QK, MIX_A), 0) >> LOG2_DK
    c = lax.broadcasted_iota(jnp.int32, (GLA_QK, MIX_A), 1) >> LOG2_DV
    return jnp.where(r == c, 1.0, 0.0).astype(F32)


def _gla_kernel(*refs, n_chunks, has_state, want_state):
    qk_ref, v_ref, lf_ref, gate_ref, gn_ref = refs[:5]
    refs = refs[5:]
    s0_ref = None
    if has_state:
        s0_ref, refs = refs[0], refs[1:]
    o_ref, refs = refs[0], refs[1:]
    sout_ref = None
    if want_state:
        sout_ref, refs = refs[0], refs[1:]
    state_ref, fwd_ref = refs

    C = GLA_CHUNK
    direction = pl.program_id(1)
    step = pl.program_id(2)
    backward = direction == 1

    @pl.when(step == 0)
    def _():
        if has_state:
            s0 = s0_ref[0, 0]
            state_ref[...] = jnp.concatenate([s0] * GLA_HEADS, axis=1) * _head_block_mask()
        else:
            state_ref[...] = jnp.zeros_like(state_ref)

    q = qk_ref[:, :GLA_QK]
    k = qk_ref[:, GLA_QK:]
    v = v_ref[...]
    lf = jnp.where(backward, lf_ref[:, GLA_QK:], lf_ref[:, :GLA_QK])

    row = lax.broadcasted_iota(jnp.int32, (C, C), 0)
    col = lax.broadcasted_iota(jnp.int32, (C, C), 1)
    sign = 1 - 2 * direction
    causal = (row - col) * sign >= 0
    cum = _split_dot(jnp.where(causal, 1.0, 0.0).astype(BF16), lf)
    total_row = jnp.where(backward, cum[0:1, :], cum[C - 1:C, :])
    mid = cum[C // 2:C // 2 + 1, :]

    state = state_ref[...]
    q_in = (q * jnp.exp(cum)).astype(BF16)
    o = _dot(q_in, state.astype(BF16))

    q_c = (q * jnp.exp(cum - mid)).astype(BF16)
    k_c = k * jnp.exp(mid - cum)
    lane_head = lax.broadcasted_iota(jnp.int32, (C, GLA_QK), 1) >> LOG2_DK
    intra = []
    for h in range(GLA_HEADS):
        k_h = jnp.where(lane_head == h, k_c, 0.0).astype(BF16)
        att = jnp.where(causal, _dot_nt(q_c, k_h), 0.0).astype(BF16)
        intra.append(_dot(att, v[:, h * GLA_DV:(h + 1) * GLA_DV]))
    o = o + jnp.concatenate(intra, axis=1)

    k_out = (k * jnp.exp(total_row - cum)).T.astype(BF16)
    decay = jnp.exp(jnp.sum(lf.T, axis=1, keepdims=True))
    mask = _head_block_mask()
    new_state = state * decay + _dot(k_out, v) * mask
    state_ref[...] = new_state

    chunk = pl.multiple_of(jnp.where(backward, n_chunks - 1 - step, step) * C, C)

    @pl.when(direction == 0)
    def _():
        fwd_ref[pl.ds(chunk, C), :] = o

    @pl.when(backward)
    def _():
        both = fwd_ref[pl.ds(chunk, C), :] + o
        normed = []
        for h in range(GLA_HEADS):
            o_h = both[:, h * GLA_DV:(h + 1) * GLA_DV]
            ms = jnp.mean(o_h * o_h, axis=-1, keepdims=True)
            normed.append(o_h * lax.rsqrt(ms + EPS) * gn_ref[...])
        o_ref[...] = (jnp.concatenate(normed, axis=1) * _silu(gate_ref[...])).astype(BF16)

    if want_state:
        @pl.when(step == n_chunks - 1)
        def _():
            s = new_state[:, :GLA_DV]
            for h in range(1, GLA_HEADS):
                s = s + new_state[:, h * GLA_DV:(h + 1) * GLA_DV]
            sout_ref[0, 0] = s


def _gla(qk, v, lf, gate, wp, layer, s0, batch, seq_len, want_state):
    n = qk.shape[0]
    n_chunks = seq_len // GLA_CHUNK
    has_state = s0 is not None

    def chunk_map(b, d, c):
        return (b * n_chunks + jnp.where(d == 0, c, n_chunks - 1 - c), 0)

    def out_map(b, d, c):
        return (b * n_chunks + jnp.where(d == 0, n_chunks - 1, n_chunks - 1 - c), 0)

    def state_map(b, d, c):
        return (b, d, 0, 0)

    in_specs = [
        pl.BlockSpec((GLA_CHUNK, 2 * GLA_QK), chunk_map),
        pl.BlockSpec((GLA_CHUNK, MIX_A), chunk_map),
        pl.BlockSpec((GLA_CHUNK, 2 * GLA_QK), chunk_map),
        pl.BlockSpec((GLA_CHUNK, MIX_A), chunk_map),
        _resident((1, GLA_DV), layer // 2),
    ]
    args = [qk, v, lf, gate, wp['gn']]
    if has_state:
        in_specs.append(pl.BlockSpec((1, 1, GLA_QK, GLA_DV), state_map))
        args.append(s0)
    out_specs = [pl.BlockSpec((GLA_CHUNK, MIX_A), out_map)]
    out_shape = [jax.ShapeDtypeStruct((n, MIX_A), BF16)]
    if want_state:
        out_specs.append(pl.BlockSpec((1, 1, GLA_QK, GLA_DV), state_map))
        out_shape.append(jax.ShapeDtypeStruct((batch, 2, GLA_QK, GLA_DV), F32))
    outs = pl.pallas_call(
        functools.partial(_gla_kernel, n_chunks=n_chunks, has_state=has_state, want_state=want_state),
        grid=(batch, 2, n_chunks),
        in_specs=in_specs,
        out_specs=out_specs,
        out_shape=out_shape,
        scratch_shapes=[pltpu.VMEM((GLA_QK, MIX_A), F32), pltpu.VMEM((seq_len, MIX_A), F32)],
        compiler_params=_params(3),
        name="gla",
    )(*args)
    return (outs[0], outs[1]) if want_state else (outs[0], None)


DFT_ROWS = 256
DFT_LOW = 32


def _dft_rows(rows, n):
    k = jnp.arange(n, dtype=jnp.int32)
    ang = ((rows[:, None] * k[None, :]) % n).astype(F32) * (2.0 * math.pi / n)
    return jnp.cos(ang), jnp.sin(ang)


def _dft_tables(n):
    c_lo, s_lo = _dft_rows(jnp.arange(DFT_LOW, dtype=jnp.int32), n)
    c_hi, s_hi = _dft_rows(jnp.arange(n // DFT_LOW, dtype=jnp.int32) * DFT_LOW, n)
    cos = c_hi[:, None, :] * c_lo[None, :, :] - s_hi[:, None, :] * s_lo[None, :, :]
    sin = s_hi[:, None, :] * c_lo[None, :, :] + c_hi[:, None, :] * s_lo[None, :, :]
    return cos.reshape(n, n), sin.reshape(n, n)


def _fourier_kernel(f_ref, cs_ref, cl_ref, sl_ref, out_ref, xc_ref, xs_ref, *, scale):
    n = f_ref.shape[0]
    for g in range(MIX_C // FOURIER_GW):
        sl = slice(g * FOURIER_GW, (g + 1) * FOURIER_GW)
        x = _dot(f_ref[:, sl], cs_ref[...])
        xc_ref[:, sl] = x[:, :FOURIER_GW].astype(BF16)
        xs_ref[:, sl] = x[:, FOURIER_GW:].astype(BF16)

    def body(r, carry):
        rows = pl.ds(pl.multiple_of(r * DFT_ROWS, DFT_ROWS), DFT_ROWS)
        re = _dot(cl_ref[rows, :], xc_ref[...]) - _dot(sl_ref[rows, :], xs_ref[...])
        out_ref[rows, :] = (re * scale).astype(BF16)
        return carry

    lax.fori_loop(0, n // DFT_ROWS, body, 0)


def _fourier_tables(seq_len):
    cc, sc = _dft_rows(jnp.arange(FOURIER_GW, dtype=jnp.int32), FOURIER_GW)
    cl, sl = _dft_tables(seq_len)
    return jnp.concatenate([cc, sc], axis=1).astype(BF16), cl.astype(BF16), sl.astype(BF16)


def _fourier(f, tables, batch, seq_len):
    n = f.shape[0]
    scale = 1.0 / math.sqrt(seq_len * FOURIER_GW)
    seq_spec = pl.BlockSpec((seq_len, MIX_C), lambda b: (b, 0))
    return pl.pallas_call(
        functools.partial(_fourier_kernel, scale=scale),
        grid=(batch,),
        in_specs=[seq_spec, _resident((FOURIER_GW, 2 * FOURIER_GW)),
                  _resident((seq_len, seq_len)), _resident((seq_len, seq_len))],
        out_specs=seq_spec,
        out_shape=jax.ShapeDtypeStruct((n, MIX_C), BF16),
        scratch_shapes=[pltpu.VMEM((seq_len, MIX_C), BF16), pltpu.VMEM((seq_len, MIX_C), BF16)],
        compiler_params=_params(1),
        name="fourier",
    )(f, *tables)


def _grid_pos_embed(n_tok):
    rows = n_tok // GRID_W
    quarter = D_MODEL // 4
    freqs = 1.0 / (10000.0 ** (jnp.arange(quarter, dtype=F32) / quarter))
    ar = jnp.arange(rows, dtype=F32)[:, None] * freqs
    ac = jnp.arange(GRID_W, dtype=F32)[:, None] * freqs

    def per_row(t):
        return jnp.broadcast_to(t[:, None, :], (rows, GRID_W, quarter))

    def per_col(t):
        return jnp.broadcast_to(t[None, :, :], (rows, GRID_W, quarter))

    pos = jnp.concatenate([per_row(jnp.sin(ar)), per_row(jnp.cos(ar)), per_col(jnp.sin(ac)), per_col(jnp.cos(ac))],
                          axis=-1)
    return pos.reshape(n_tok, D_MODEL)


def _prep_weights(norm_g, final_g, w_mlp1, w_mlp2, w_in_even, w_a2, b_a2, gla_norm_g, pool_w, pool_s,
                  w_out_even, w_in_odd, conv_w, conv_b, w_out_odd):
    o4 = 2 * GLA_QK + 2 * MIX_A
    o5 = o4 + 2 * GATE_RANK
    w_even = jnp.concatenate(
        [w_in_even[:, :, :o4], w_in_even[:, :, o5:], w_in_even[:, :, o4:o5],
         jnp.zeros((N_EVEN, D_MODEL, LANES - 2 * GATE_RANK), F32)], axis=-1).astype(BF16)
    wgate = jnp.zeros((N_EVEN, LANES, 2 * GLA_QK), F32)
    wgate = wgate.at[:, :GATE_RANK, :GLA_QK].set(w_a2[:, 0])
    wgate = wgate.at[:, GATE_RANK:2 * GATE_RANK, GLA_QK:].set(w_a2[:, 1])
    return dict(
        norm_g=norm_g.reshape(DEPTH * 2, 1, D_MODEL),
        final_g=final_g.reshape(1, D_MODEL),
        w1=w_mlp1.astype(BF16), w2=w_mlp2.astype(BF16),
        w_even=w_even, wgate=wgate.astype(BF16), bgate=b_a2.reshape(N_EVEN, 1, 2 * GLA_QK),
        gn=gla_norm_g.reshape(N_EVEN, 1, GLA_DV),
        pool_w=pool_w.astype(BF16), pool_s=pool_s.reshape(N_EVEN, 1, MIX_B),
        w_out_even=w_out_even.astype(BF16),
        w_odd=w_in_odd.astype(BF16), conv_w=conv_w, conv_b=conv_b.reshape(-1, 1, MIX_D),
        w_out_odd=w_out_odd.astype(BF16),
    )


def _trunk(x, pos, mods, gla_init, shared_mod, want_state, wp):
    batch, seq_len, _ = x.shape
    x = x.reshape(batch * seq_len, D_MODEL)
    tables = _fourier_tables(seq_len)
    states = []
    for l in range(DEPTH):
        j = l // 2
        if l % 2 == 0:
            qk, v, gate, lf, b = _in_even(x, pos, mods, wp, l, seq_len, shared_mod)
            s0 = None if gla_init is None else gla_init[:, j].reshape(batch, 2, GLA_QK, GLA_DV)
            a, s = _gla(qk, v, lf, gate, wp, l, s0, batch, seq_len, want_state)
            if want_state:
                states.append(s.reshape(batch, 2, GLA_HEADS, GLA_DK, GLA_DV))
        else:
            f, b = _in_odd(x, mods, wp, l, seq_len, shared_mod)
            a = _fourier(f, tables, batch, seq_len)
        x = _post(x, pos, a, b, mods, wp, l, seq_len, shared_mod)
        pos = None
    y = x.reshape(batch, seq_len, D_MODEL)
    return y, (jnp.stack(states, axis=1) if want_state else None)


def kernel(x_prompt, x_sample, state_gla, c, c_ctx, norm_g, final_g, w_ada, b_ada, w_mlp1, w_mlp2,
           w_in_even, w_a2, b_a2, gla_norm_g, pool_w, pool_s, w_out_even,
           w_in_odd, conv_w, conv_b, w_out_odd):
    wp = _prep_weights(norm_g, final_g, w_mlp1, w_mlp2, w_in_even, w_a2, b_a2, gla_norm_g, pool_w, pool_s,
                       w_out_even, w_in_odd, conv_w, conv_b, w_out_odd)
    n_dec = c.shape[0]
    cond = jnp.concatenate([c_ctx[None, :], c, jnp.zeros((COND_ROWS - 1 - n_dec, D_MODEL), F32)], axis=0)
    mods = _ada(cond, w_ada, b_ada).reshape(DEPTH * COND_ROWS, 1, 6 * D_MODEL)

    y_prompt, new_state = _trunk(x_prompt, None, mods, None, True, True, wp)
    pos = _grid_pos_embed(x_sample.shape[1]).astype(x_sample.dtype)
    y_sample, _ = _trunk(x_sample, pos, mods, state_gla, False, False, wp)
    return (y_prompt, y_sample, new_state.astype(x_prompt.dtype))
```

```python
import functools
import math

import jax
import jax.numpy as jnp
from jax import lax
from jax.experimental import pallas as pl
from jax.experimental.pallas import tpu as pltpu

D_MODEL = 1024
DEPTH = 4
N_EVEN = 2
GRID_W = 64
MIX_A = 512
MIX_B = 512
GLA_HEADS = 4
GLA_DV = 128
GLA_DK = 64
GLA_QK = 256
LOG2_DK = 6
LOG2_DV = 7
GATE_RANK = 16
GATE_TAU = 16.0
POOL_WINDOWS = (2, 4, 8, 16)
POOL_GW = 128
MIX_C = 512
MIX_D = 512
FOURIER_GW = 128
D_FF = 4096
EPS = 1e-6

LANES = 128
SUBLANES = 8
TOKEN_TILE = 512
GLA_CHUNK = 256
EVEN_IN_PAD = 2048 + LANES
VMEM_LIMIT = 56 * 1024 * 1024

F32 = jnp.float32
BF16 = jnp.bfloat16


def _dot(a, b):
    return jnp.dot(a, b, preferred_element_type=F32)


def _dot_nt(a, b):
    return lax.dot_general(a, b, (((1,), (1,)), ((), ())), preferred_element_type=F32)


def _split_dot(m, x):
    hi = x.astype(BF16)
    lo = (x - hi.astype(F32)).astype(BF16)
    return _dot(m, hi) + _dot(m, lo)


def _silu(x):
    return x * (1.0 / (1.0 + jnp.exp(-x)))


def _rmsnorm(x, g):
    ms = jnp.mean(x * x, axis=-1, keepdims=True)
    return (x * lax.rsqrt(ms + EPS)) * g


def _params(n_axes):
    return pltpu.CompilerParams(dimension_semantics=("arbitrary",) * n_axes, vmem_limit_bytes=VMEM_LIMIT)


def _resident(shape, layer=None):
    zeros = (0,) * len(shape)
    if layer is None:
        return pl.BlockSpec(shape, lambda *_: zeros, pipeline_mode=pl.Buffered(1))
    return pl.BlockSpec((None,) + tuple(shape), lambda *_: (layer,) + zeros, pipeline_mode=pl.Buffered(1))


ADA_COLS = 1536
COND_ROWS = SUBLANES


def _ada_kernel(cond_ref, w_ref, b_ref, out_ref):
    s = _silu(cond_ref[...]).astype(BF16)
    out_ref[0] = _dot(s, w_ref[0].astype(BF16)) + b_ref[0]


def _ada(cond, w_ada, b_ada):
    n_out = w_ada.shape[-1]
    return pl.pallas_call(
        _ada_kernel,
        grid=(DEPTH, n_out // ADA_COLS),
        in_specs=[
            pl.BlockSpec((COND_ROWS, D_MODEL), lambda l, j: (0, 0)),
            pl.BlockSpec((1, D_MODEL, ADA_COLS), lambda l, j: (l, 0, j)),
            pl.BlockSpec((1, 1, ADA_COLS), lambda l, j: (l, 0, j)),
        ],
        out_specs=pl.BlockSpec((1, COND_ROWS, ADA_COLS), lambda l, j: (l, 0, j)),
        out_shape=jax.ShapeDtypeStruct((DEPTH, COND_ROWS, n_out), F32),
        compiler_params=_params(2),
        name="ada_mod",
    )(cond, w_ada, b_ada.reshape(DEPTH, 1, n_out))


def _modulated_norm(x, mod_ref, g_ref, which):
    shift = mod_ref[0, :, (3 * which) * D_MODEL:(3 * which + 1) * D_MODEL]
    scale = mod_ref[0, :, (3 * which + 1) * D_MODEL:(3 * which + 2) * D_MODEL]
    return _rmsnorm(x, g_ref[...]) * (1.0 + scale) + shift


def _load_x(x_ref, pos_ref):
    return x_ref[...] if pos_ref is None else x_ref[...] + pos_ref[...]


HALO = SUBLANES
HALO_TILE = TOKEN_TILE + 2 * HALO
MAIN = slice(HALO, HALO + TOKEN_TILE)


def _norm_with_halo(refs, has_pos, seq_len, pos_cols):
    x_refs, refs = refs[:3], refs[3:]
    x = jnp.concatenate([r[...] for r in x_refs], axis=0)
    if has_pos:
        pos_refs, refs = refs[:3], refs[3:]
        x = x + jnp.concatenate([r[...] for r in pos_refs], axis=0)
    mod_ref, g_ref, w_ref, refs = refs[0], refs[1], refs[2], refs[3:]
    h = _modulated_norm(x, mod_ref, g_ref, 0).astype(BF16)
    h_main = h[MAIN]

    def project(lo, hi, with_halo=False):
        return _dot(h if with_halo else h_main, w_ref[:, lo:hi])

    row = lax.broadcasted_iota(jnp.int32, (HALO_TILE, pos_cols), 0)
    seq_pos = (pl.program_id(0) * TOKEN_TILE - HALO + row) & (seq_len - 1)
    return refs, project, seq_pos


def _shift_down(x, s, seq_pos):
    return jnp.where(seq_pos >= s, pltpu.roll(x, s, axis=0), 0.0)


def _shift_up(x, s, seq_pos, seq_len):
    return jnp.where(seq_pos < seq_len - s, pltpu.roll(x, x.shape[0] - s, axis=0), 0.0)


def _in_even_kernel(*refs, has_pos, seq_len):
    refs, project, seq_pos = _norm_with_halo(refs, has_pos, seq_len, POOL_GW)
    wgate_ref, bgate_ref, pw_ref, ps_ref, qk_ref, v_ref, gate_ref, lf_ref, pool_ref = refs
    p = project(0, EVEN_IN_PAD, with_halo=True)
    qk_ref[:, :GLA_QK] = p[MAIN, :GLA_QK] * (GLA_DK ** -0.5)
    qk_ref[:, GLA_QK:] = p[MAIN, GLA_QK:2 * GLA_QK]
    v_ref[...] = p[MAIN, 512:1024].astype(BF16)
    gate_ref[...] = p[MAIN, 1024:1536]
    pre = _dot(p[MAIN, 2048:].astype(BF16), wgate_ref[...]) + bgate_ref[...]
    log_sig = jnp.minimum(pre, 0.0) - jnp.log(1.0 + jnp.exp(-jnp.abs(pre)))
    lf_ref[...] = log_sig / GATE_TAU

    for gi, w in enumerate(POOL_WINDOWS):
        sl = slice(gi * POOL_GW, (gi + 1) * POOL_GW)
        u = p[:, 1536 + gi * POOL_GW:1536 + (gi + 1) * POOL_GW]
        half = w // 2
        trail, lead, m = u, u, 1
        while m < half:
            trail = trail + _shift_down(trail, m, seq_pos)
            lead = lead + _shift_up(lead, m, seq_pos, seq_len)
            m *= 2
        window = _shift_down(trail, 1, seq_pos) + lead
        cnt = (jnp.minimum(seq_pos + (w - half), seq_len) - jnp.maximum(seq_pos - half, 0)).astype(F32)
        pooled = (window / cnt - u)[MAIN]
        y = _dot(pooled.astype(BF16), pw_ref[gi])
        pool_ref[:, sl] = (y * ps_ref[:, sl]).astype(BF16)


def _in_odd_kernel(*refs, seq_len):
    refs, project, seq_pos = _norm_with_halo(refs, False, seq_len, MIX_D)
    cw_ref, cb_ref, f_ref, conv_ref = refs
    f_ref[...] = project(0, MIX_C).astype(BF16)
    z = project(MIX_C + 2 * MIX_D, MIX_C + 3 * MIX_D, with_halo=True) * project(MIX_C, MIX_C + MIX_D, with_halo=True)
    conv = cb_ref[...] + _shift_down(z, 1, seq_pos)[MAIN] * cw_ref[0:1, :]
    conv = conv + z[MAIN] * cw_ref[1:2, :]
    conv = conv + _shift_up(z, 1, seq_pos, seq_len)[MAIN] * cw_ref[2:3, :]
    conv_ref[...] = (project(MIX_C + MIX_D, MIX_C + 2 * MIX_D) * conv).astype(BF16)


def _mod_spec(layer, seq_len, shared):
    base = layer * COND_ROWS
    if shared:
        return pl.BlockSpec((1, 1, 6 * D_MODEL), lambda i: (base, 0, 0))
    return pl.BlockSpec((1, 1, 6 * D_MODEL), lambda i: (base + 1 + i * TOKEN_TILE // seq_len, 0, 0))


def _tile_spec(cols):
    return pl.BlockSpec((TOKEN_TILE, cols), lambda i: (i, 0))


def _pos_spec(seq_len):
    tiles = seq_len // TOKEN_TILE
    return pl.BlockSpec((TOKEN_TILE, D_MODEL), lambda i: (i % tiles, 0))


def _halo_specs(n_rows, period_tiles):
    per_tile = TOKEN_TILE // HALO
    last = n_rows // HALO - 1
    return [
        pl.BlockSpec((HALO, D_MODEL), lambda i: (jnp.maximum((i % period_tiles) * per_tile - 1, 0), 0)),
        pl.BlockSpec((TOKEN_TILE, D_MODEL), lambda i: (i % period_tiles, 0)),
        pl.BlockSpec((HALO, D_MODEL), lambda i: (jnp.minimum((i % period_tiles + 1) * per_tile, last), 0)),
    ]


def _in_even(x, pos, mods, wp, layer, seq_len, shared):
    n = x.shape[0]
    j = layer // 2
    outs = [(512, F32), (512, BF16), (512, F32), (512, F32), (MIX_B, BF16)]
    has_pos = pos is not None
    in_specs = _halo_specs(n, n // TOKEN_TILE) + (_halo_specs(seq_len, seq_len // TOKEN_TILE) if has_pos else []) + [
        _mod_spec(layer, seq_len, shared), _resident((1, D_MODEL), 2 * layer),
        _resident((D_MODEL, EVEN_IN_PAD), j), _resident((LANES, 2 * GLA_QK), j), _resident((1, 2 * GLA_QK), j),
        _resident((len(POOL_WINDOWS), POOL_GW, POOL_GW), j), _resident((1, MIX_B), j),
    ]
    args = [x] * 3 + ([pos] * 3 if has_pos else []) + [
        mods, wp['norm_g'], wp['w_even'], wp['wgate'], wp['bgate'], wp['pool_w'], wp['pool_s']]
    return pl.pallas_call(
        functools.partial(_in_even_kernel, has_pos=has_pos, seq_len=seq_len),
        grid=(n // TOKEN_TILE,),
        in_specs=in_specs,
        out_specs=[_tile_spec(c) for c, _ in outs],
        out_shape=[jax.ShapeDtypeStruct((n, c), dt) for c, dt in outs],
        compiler_params=_params(1),
        name="in_even",
    )(*args)


def _in_odd(x, mods, wp, layer, seq_len, shared):
    n = x.shape[0]
    j = layer // 2
    outs = [(MIX_C, BF16), (MIX_D, BF16)]
    return pl.pallas_call(
        functools.partial(_in_odd_kernel, seq_len=seq_len),
        grid=(n // TOKEN_TILE,),
        in_specs=_halo_specs(n, n // TOKEN_TILE) + [
            _mod_spec(layer, seq_len, shared), _resident((1, D_MODEL), 2 * layer),
            _resident((D_MODEL, MIX_C + 3 * MIX_D), j),
            _resident(wp['conv_w'].shape[1:], j), _resident((1, MIX_D), j),
        ],
        out_specs=[_tile_spec(c) for c, _ in outs],
        out_shape=[jax.ShapeDtypeStruct((n, c), dt) for c, dt in outs],
        compiler_params=_params(1),
        name="in_odd",
    )(x, x, x, mods, wp['norm_g'], wp['w_odd'], wp['conv_w'], wp['conv_b'])


FF_BLOCK = 1024


def _post_kernel(*refs, has_pos, final):
    x_ref, refs = refs[0], refs[1:]
    pos_ref = None
    if has_pos:
        pos_ref, refs = refs[0], refs[1:]
    a_ref, b_ref, mod_ref, g_ref, wo_ref, w1_ref, w2_ref, fg_ref, out_ref = refs
    half = D_MODEL // 2
    y = _dot(a_ref[...], wo_ref[:half, :]) + _dot(b_ref[...], wo_ref[half:, :])
    gate1 = mod_ref[0, :, 2 * D_MODEL:3 * D_MODEL]
    gate2 = mod_ref[0, :, 5 * D_MODEL:6 * D_MODEL]
    x1 = _load_x(x_ref, pos_ref) + gate1 * y
    h = _modulated_norm(x1, mod_ref, g_ref, 1).astype(BF16)
    acc = jnp.zeros_like(x1)
    for c in range(D_FF // FF_BLOCK):
        a = jnp.maximum(_dot(h, w1_ref[:, c * FF_BLOCK:(c + 1) * FF_BLOCK]), 0.0)
        acc = acc + _dot((a * a).astype(BF16), w2_ref[c * FF_BLOCK:(c + 1) * FF_BLOCK, :])
    x2 = x1 + gate2 * acc
    if final:
        x2 = _rmsnorm(x2, fg_ref[...])
    out_ref[...] = x2


def _post(x, pos, a, b, mods, wp, layer, seq_len, shared):
    n = x.shape[0]
    j = layer // 2
    final = layer == DEPTH - 1
    has_pos = pos is not None
    w_out = wp['w_out_even'] if layer % 2 == 0 else wp['w_out_odd']
    in_specs = [_tile_spec(D_MODEL)] + ([_pos_spec(seq_len)] if has_pos else []) + [
        _tile_spec(D_MODEL // 2), _tile_spec(D_MODEL // 2),
        _mod_spec(layer, seq_len, shared), _resident((1, D_MODEL), 2 * layer + 1),
        _resident((D_MODEL, D_MODEL), j), _resident((D_MODEL, D_FF), layer), _resident((D_FF, D_MODEL), layer),
        _resident((1, D_MODEL)),
    ]
    args = [x] + ([pos] if has_pos else []) + [a, b, mods, wp['norm_g'], w_out, wp['w1'], wp['w2'], wp['final_g']]
    return pl.pallas_call(
        functools.partial(_post_kernel, has_pos=has_pos, final=final),
        grid=(n // TOKEN_TILE,),
        in_specs=in_specs,
        out_specs=_tile_spec(D_MODEL),
        out_shape=jax.ShapeDtypeStruct((n, D_MODEL), F32),
        compiler_params=_params(1),
        name="post_final" if final else "post",
    )(*args)


def _head_block_mask():
    r = lax.broadcasted_iota(jnp.int32, (GLA_QK, MIX_A), 0) >> LOG2_DK
    c = lax.broadcasted_iota(jnp.int32, (GLA_QK, MIX_A), 1) >> LOG2_DV
    return jnp.where(r == c, 1.0, 0.0).astype(F32)


def _gla_chunk(qk_ref, v_ref, lf_ref, state, backward):
    C = GLA_CHUNK
    q = qk_ref[:, :GLA_QK]
    k = qk_ref[:, GLA_QK:]
    v = v_ref[...]
    lf = lf_ref[...]

    row = lax.broadcasted_iota(jnp.int32, (C, C), 0)
    col = lax.broadcasted_iota(jnp.int32, (C, C), 1)
    causal = (col >= row) if backward else (col <= row)
    cum = _split_dot(jnp.where(causal, 1.0, 0.0).astype(BF16), lf)
    total_row = cum[0:1, :] if backward else cum[C - 1:C, :]
    mid = cum[C // 2:C // 2 + 1, :]

    q_in = (q * jnp.exp(cum)).astype(BF16)
    o = _dot(q_in, state.astype(BF16))

    q_c = (q * jnp.exp(cum - mid)).astype(BF16)
    k_c = k * jnp.exp(mid - cum)
    lane_head = lax.broadcasted_iota(jnp.int32, (C, GLA_QK), 1) >> LOG2_DK
    intra = []
    for h in range(GLA_HEADS):
        k_h = jnp.where(lane_head == h, k_c, 0.0).astype(BF16)
        att = jnp.where(causal, _dot_nt(q_c, k_h), 0.0).astype(BF16)
        intra.append(_dot(att, v[:, h * GLA_DV:(h + 1) * GLA_DV]))
    o = o + jnp.concatenate(intra, axis=1)

    k_out = (k * jnp.exp(total_row - cum)).T.astype(BF16)
    decay = jnp.exp(jnp.sum(lf.T, axis=1, keepdims=True))
    mask = _head_block_mask()
    new_state = state * decay + _dot(k_out, v) * mask
    return o, new_state


def _gla_kernel(*refs, n_chunks, has_state, want_state):
    (qkf_ref, vf_ref, lff_ref, qkb_ref, vb_ref, lfb_ref, gate_ref, gn_ref), refs = refs[:8], refs[8:]
    s0_ref = None
    if has_state:
        s0_ref, refs = refs[0], refs[1:]
    o_ref, refs = refs[0], refs[1:]
    sout_ref = None
    if want_state:
        sout_ref, refs = refs[0], refs[1:]
    state_ref, part_f_ref, part_b_ref = refs

    C = GLA_CHUNK
    step = pl.program_id(1)

    @pl.when(step == 0)
    def _():
        for d in range(2):
            if has_state:
                state_ref[d] = jnp.concatenate([s0_ref[0, d]] * GLA_HEADS, axis=1) * _head_block_mask()
            else:
                state_ref[d] = jnp.zeros((GLA_QK, MIX_A), F32)

    o_f, state_f = _gla_chunk(qkf_ref, vf_ref, lff_ref, state_ref[0], backward=False)
    o_b, state_b = _gla_chunk(qkb_ref, vb_ref, lfb_ref, state_ref[1], backward=True)
    state_ref[0] = state_f
    state_ref[1] = state_b

    rows_f = pl.ds(pl.multiple_of(step * C, C), C)
    rows_b = pl.ds(pl.multiple_of((n_chunks - 1 - step) * C, C), C)
    part_f_ref[rows_f, :] = o_f
    part_b_ref[rows_b, :] = o_b

    def finish(rows):
        both = part_f_ref[rows, :] + part_b_ref[rows, :]
        normed = []
        for h in range(GLA_HEADS):
            o_h = both[:, h * GLA_DV:(h + 1) * GLA_DV]
            ms = jnp.mean(o_h * o_h, axis=-1, keepdims=True)
            normed.append(o_h * lax.rsqrt(ms + EPS) * gn_ref[...])
        o_ref[rows, :] = (jnp.concatenate(normed, axis=1) * _silu(gate_ref[rows, :])).astype(BF16)

    @pl.when(2 * step >= n_chunks - 1)
    def _():
        finish(rows_f)

    @pl.when(2 * step > n_chunks - 1)
    def _():
        finish(rows_b)

    if want_state:
        @pl.when(step == n_chunks - 1)
        def _():
            for d, st in enumerate((state_f, state_b)):
                s = st[:, :GLA_DV]
                for h in range(1, GLA_HEADS):
                    s = s + st[:, h * GLA_DV:(h + 1) * GLA_DV]
                sout_ref[0, d] = s


def _gla(qk, v, lf, gate, wp, layer, s0, batch, seq_len, want_state):
    n = qk.shape[0]
    n_chunks = seq_len // GLA_CHUNK
    has_state = s0 is not None

    def fwd_map(col):
        return lambda b, s: (b * n_chunks + s, col)

    def bwd_map(col):
        return lambda b, s: (b * n_chunks + n_chunks - 1 - s, col)

    seq_spec = pl.BlockSpec((seq_len, MIX_A), lambda b, s: (b, 0))
    state_spec = pl.BlockSpec((1, 2, GLA_QK, GLA_DV), lambda b, s: (b, 0, 0, 0))
    in_specs = []
    for chunk_map in (fwd_map, bwd_map):
        in_specs += [
            pl.BlockSpec((GLA_CHUNK, 2 * GLA_QK), chunk_map(0)),
            pl.BlockSpec((GLA_CHUNK, MIX_A), chunk_map(0)),
            pl.BlockSpec((GLA_CHUNK, GLA_QK), chunk_map(0 if chunk_map is fwd_map else 1)),
        ]
    in_specs += [seq_spec, _resident((1, GLA_DV), layer // 2)]
    args = [qk, v, lf, qk, v, lf, gate, wp['gn']]
    if has_state:
        in_specs.append(state_spec)
        args.append(s0)
    out_specs = [seq_spec]
    out_shape = [jax.ShapeDtypeStruct((n, MIX_A), BF16)]
    if want_state:
        out_specs.append(state_spec)
        out_shape.append(jax.ShapeDtypeStruct((batch, 2, GLA_QK, GLA_DV), F32))
    outs = pl.pallas_call(
        functools.partial(_gla_kernel, n_chunks=n_chunks, has_state=has_state, want_state=want_state),
        grid=(batch, n_chunks),
        in_specs=in_specs,
        out_specs=out_specs,
        out_shape=out_shape,
        scratch_shapes=[pltpu.VMEM((2, GLA_QK, MIX_A), F32), pltpu.VMEM((seq_len, MIX_A), F32),
                        pltpu.VMEM((seq_len, MIX_A), F32)],
        compiler_params=_params(2),
        name="gla",
    )(*args)
    return (outs[0], outs[1]) if want_state else (outs[0], None)


DFT_ROWS = 256
DFT_LOW = 32


def _dft_rows(rows, n):
    k = jnp.arange(n, dtype=jnp.int32)
    ang = ((rows[:, None] * k[None, :]) % n).astype(F32) * (2.0 * math.pi / n)
    return jnp.cos(ang), jnp.sin(ang)


def _dft_tables(n):
    c_lo, s_lo = _dft_rows(jnp.arange(DFT_LOW, dtype=jnp.int32), n)
    c_hi, s_hi = _dft_rows(jnp.arange(n // DFT_LOW, dtype=jnp.int32) * DFT_LOW, n)
    cos = c_hi[:, None, :] * c_lo[None, :, :] - s_hi[:, None, :] * s_lo[None, :, :]
    sin = s_hi[:, None, :] * c_lo[None, :, :] + c_hi[:, None, :] * s_lo[None, :, :]
    return cos.reshape(n, n), sin.reshape(n, n)


def _fourier_kernel(f_ref, cs_ref, cl_ref, sl_ref, out_ref, xc_ref, xs_ref, *, scale):
    n = f_ref.shape[0]
    for g in range(MIX_C // FOURIER_GW):
        sl = slice(g * FOURIER_GW, (g + 1) * FOURIER_GW)
        x = _dot(f_ref[:, sl], cs_ref[...])
        xc_ref[:, sl] = x[:, :FOURIER_GW].astype(BF16)
        xs_ref[:, sl] = x[:, FOURIER_GW:].astype(BF16)

    def body(r, carry):
        rows = pl.ds(pl.multiple_of(r * DFT_ROWS, DFT_ROWS), DFT_ROWS)
        re = _dot(cl_ref[rows, :], xc_ref[...]) - _dot(sl_ref[rows, :], xs_ref[...])
        out_ref[rows, :] = (re * scale).astype(BF16)
        return carry

    lax.fori_loop(0, n // DFT_ROWS, body, 0)


def _fourier_tables(seq_len):
    cc, sc = _dft_rows(jnp.arange(FOURIER_GW, dtype=jnp.int32), FOURIER_GW)
    cl, sl = _dft_tables(seq_len)
    return jnp.concatenate([cc, sc], axis=1).astype(BF16), cl.astype(BF16), sl.astype(BF16)


def _fourier(f, tables, batch, seq_len):
    n = f.shape[0]
    scale = 1.0 / math.sqrt(seq_len * FOURIER_GW)
    seq_spec = pl.BlockSpec((seq_len, MIX_C), lambda b: (b, 0))
    return pl.pallas_call(
        functools.partial(_fourier_kernel, scale=scale),
        grid=(batch,),
        in_specs=[seq_spec, _resident((FOURIER_GW, 2 * FOURIER_GW)),
                  _resident((seq_len, seq_len)), _resident((seq_len, seq_len))],
        out_specs=seq_spec,
        out_shape=jax.ShapeDtypeStruct((n, MIX_C), BF16),
        scratch_shapes=[pltpu.VMEM((seq_len, MIX_C), BF16), pltpu.VMEM((seq_len, MIX_C), BF16)],
        compiler_params=_params(1),
        name="fourier",
    )(f, *tables)


def _grid_pos_embed(n_tok):
    rows = n_tok // GRID_W
    quarter = D_MODEL // 4
    freqs = 1.0 / (10000.0 ** (jnp.arange(quarter, dtype=F32) / quarter))
    ar = jnp.arange(rows, dtype=F32)[:, None] * freqs
    ac = jnp.arange(GRID_W, dtype=F32)[:, None] * freqs

    def per_row(t):
        return jnp.broadcast_to(t[:, None, :], (rows, GRID_W, quarter))

    def per_col(t):
        return jnp.broadcast_to(t[None, :, :], (rows, GRID_W, quarter))

    pos = jnp.concatenate([per_row(jnp.sin(ar)), per_row(jnp.cos(ar)), per_col(jnp.sin(ac)), per_col(jnp.cos(ac))],
                          axis=-1)
    return pos.reshape(n_tok, D_MODEL)


def _prep_weights(norm_g, final_g, w_mlp1, w_mlp2, w_in_even, w_a2, b_a2, gla_norm_g, pool_w, pool_s,
                  w_out_even, w_in_odd, conv_w, conv_b, w_out_odd):
    o4 = 2 * GLA_QK + 2 * MIX_A
    o5 = o4 + 2 * GATE_RANK
    w_even = jnp.concatenate(
        [w_in_even[:, :, :o4], w_in_even[:, :, o5:], w_in_even[:, :, o4:o5],
         jnp.zeros((N_EVEN, D_MODEL, LANES - 2 * GATE_RANK), F32)], axis=-1).astype(BF16)
    wgate = jnp.zeros((N_EVEN, LANES, 2 * GLA_QK), F32)
    wgate = wgate.at[:, :GATE_RANK, :GLA_QK].set(w_a2[:, 0])
    wgate = wgate.at[:, GATE_RANK:2 * GATE_RANK, GLA_QK:].set(w_a2[:, 1])
    return dict(
        norm_g=norm_g.reshape(DEPTH * 2, 1, D_MODEL),
        final_g=final_g.reshape(1, D_MODEL),
        w1=w_mlp1.astype(BF16), w2=w_mlp2.astype(BF16),
        w_even=w_even, wgate=wgate.astype(BF16), bgate=b_a2.reshape(N_EVEN, 1, 2 * GLA_QK),
        gn=gla_norm_g.reshape(N_EVEN, 1, GLA_DV),
        pool_w=pool_w.astype(BF16), pool_s=pool_s.reshape(N_EVEN, 1, MIX_B),
        w_out_even=w_out_even.astype(BF16),
        w_odd=w_in_odd.astype(BF16), conv_w=conv_w, conv_b=conv_b.reshape(-1, 1, MIX_D),
        w_out_odd=w_out_odd.astype(BF16),
    )


def _trunk(x, pos, mods, gla_init, shared_mod, want_state, wp):
    batch, seq_len, _ = x.shape
    x = x.reshape(batch * seq_len, D_MODEL)
    tables = _fourier_tables(seq_len)
    states = []
    for l in range(DEPTH):
        j = l // 2
        if l % 2 == 0:
            qk, v, gate, lf, b = _in_even(x, pos, mods, wp, l, seq_len, shared_mod)
            s0 = None if gla_init is None else gla_init[:, j].reshape(batch, 2, GLA_QK, GLA_DV)
            a, s = _gla(qk, v, lf, gate, wp, l, s0, batch, seq_len, want_state)
            if want_state:
                states.append(s.reshape(batch, 2, GLA_HEADS, GLA_DK, GLA_DV))
        else:
            f, b = _in_odd(x, mods, wp, l, seq_len, shared_mod)
            a = _fourier(f, tables, batch, seq_len)
        x = _post(x, pos, a, b, mods, wp, l, seq_len, shared_mod)
        pos = None
    y = x.reshape(batch, seq_len, D_MODEL)
    return y, (jnp.stack(states, axis=1) if want_state else None)


def kernel(x_prompt, x_sample, state_gla, c, c_ctx, norm_g, final_g, w_ada, b_ada, w_mlp1, w_mlp2,
           w_in_even, w_a2, b_a2, gla_norm_g, pool_w, pool_s, w_out_even,
           w_in_odd, conv_w, conv_b, w_out_odd):
    wp = _prep_weights(norm_g, final_g, w_mlp1, w_mlp2, w_in_even, w_a2, b_a2, gla_norm_g, pool_w, pool_s,
                       w_out_even, w_in_odd, conv_w, conv_b, w_out_odd)
    n_dec = c.shape[0]
    cond = jnp.concatenate([c_ctx[None, :], c, jnp.zeros((COND_ROWS - 1 - n_dec, D_MODEL), F32)], axis=0)
    mods = _ada(cond, w_ada, b_ada).reshape(DEPTH * COND_ROWS, 1, 6 * D_MODEL)

    y_prompt, new_state = _trunk(x_prompt, None, mods, None, True, True, wp)
    pos = _grid_pos_embed(x_sample.shape[1]).astype(x_sample.dtype)
    y_sample, _ = _trunk(x_sample, pos, mods, state_gla, False, False, wp)
    return (y_prompt, y_sample, new_state.astype(x_prompt.dtype))
```

```python
import functools
import math

import jax
import jax.numpy as jnp
from jax import lax
from jax.experimental import pallas as pl
from jax.experimental.pallas import tpu as pltpu

D_MODEL = 1024
DEPTH = 4
N_EVEN = 2
GRID_W = 64
MIX_A = 512
MIX_B = 512
GLA_HEADS = 4
GLA_DV = 128
GLA_DK = 64
GLA_QK = 256
LOG2_DK = 6
LOG2_DV = 7
GATE_RANK = 16
GATE_TAU = 16.0
POOL_WINDOWS = (2, 4, 8, 16)
POOL_GW = 128
MIX_C = 512
MIX_D = 512
FOURIER_GW = 128
D_FF = 4096
EPS = 1e-6

LANES = 128
SUBLANES = 8
TOKEN_TILE = 512
GLA_CHUNK = 256
EVEN_IN_PAD = 2048 + LANES
VMEM_LIMIT = 56 * 1024 * 1024

F32 = jnp.float32
BF16 = jnp.bfloat16


def _dot(a, b):
    return jnp.dot(a, b, preferred_element_type=F32)


def _dot_nt(a, b):
    return lax.dot_general(a, b, (((1,), (1,)), ((), ())), preferred_element_type=F32)


def _split_dot(m, x):
    hi = x.astype(BF16)
    lo = (x - hi.astype(F32)).astype(BF16)
    return _dot(m, hi) + _dot(m, lo)


def _silu(x):
    return x * (1.0 / (1.0 + jnp.exp(-x)))


def _rmsnorm(x, g):
    ms = jnp.mean(x * x, axis=-1, keepdims=True)
    return (x * lax.rsqrt(ms + EPS)) * g


def _params(n_axes):
    return pltpu.CompilerParams(dimension_semantics=("arbitrary",) * n_axes, vmem_limit_bytes=VMEM_LIMIT)


def _resident(shape, layer=None):
    zeros = (0,) * len(shape)
    if layer is None:
        return pl.BlockSpec(shape, lambda *_: zeros, pipeline_mode=pl.Buffered(1))
    return pl.BlockSpec((None,) + tuple(shape), lambda *_: (layer,) + zeros, pipeline_mode=pl.Buffered(1))


ADA_COLS = 1536
COND_ROWS = SUBLANES


def _ada_kernel(cond_ref, w_ref, b_ref, out_ref):
    s = _silu(cond_ref[...]).astype(BF16)
    out_ref[0] = _dot(s, w_ref[0].astype(BF16)) + b_ref[0]


def _ada(cond, w_ada, b_ada):
    n_out = w_ada.shape[-1]
    return pl.pallas_call(
        _ada_kernel,
        grid=(DEPTH, n_out // ADA_COLS),
        in_specs=[
            pl.BlockSpec((COND_ROWS, D_MODEL), lambda l, j: (0, 0)),
            pl.BlockSpec((1, D_MODEL, ADA_COLS), lambda l, j: (l, 0, j)),
            pl.BlockSpec((1, 1, ADA_COLS), lambda l, j: (l, 0, j)),
        ],
        out_specs=pl.BlockSpec((1, COND_ROWS, ADA_COLS), lambda l, j: (l, 0, j)),
        out_shape=jax.ShapeDtypeStruct((DEPTH, COND_ROWS, n_out), F32),
        compiler_params=_params(2),
        name="ada_mod",
    )(cond, w_ada, b_ada.reshape(DEPTH, 1, n_out))


def _modulated_norm(x, mod_ref, g_ref, which):
    shift = mod_ref[0, :, (3 * which) * D_MODEL:(3 * which + 1) * D_MODEL]
    scale = mod_ref[0, :, (3 * which + 1) * D_MODEL:(3 * which + 2) * D_MODEL]
    return _rmsnorm(x, g_ref[...]) * (1.0 + scale) + shift


def _load_x(x_ref, pos_ref):
    return x_ref[...] if pos_ref is None else x_ref[...] + pos_ref[...]


HALO = SUBLANES
HALO_TILE = TOKEN_TILE + 2 * HALO
MAIN = slice(HALO, HALO + TOKEN_TILE)


def _norm_with_halo(refs, has_pos, seq_len, pos_cols):
    x_refs, refs = refs[:3], refs[3:]
    x = jnp.concatenate([r[...] for r in x_refs], axis=0)
    if has_pos:
        pos_refs, refs = refs[:3], refs[3:]
        x = x + jnp.concatenate([r[...] for r in pos_refs], axis=0)
    mod_ref, g_ref, w_ref, refs = refs[0], refs[1], refs[2], refs[3:]
    h = _modulated_norm(x, mod_ref, g_ref, 0).astype(BF16)
    h_main = h[MAIN]

    def project(lo, hi, with_halo=False):
        return _dot(h if with_halo else h_main, w_ref[:, lo:hi])

    row = lax.broadcasted_iota(jnp.int32, (HALO_TILE, pos_cols), 0)
    seq_pos = (pl.program_id(0) * TOKEN_TILE - HALO + row) & (seq_len - 1)
    return refs, project, seq_pos


def _shift_down(x, s, seq_pos):
    return jnp.where(seq_pos >= s, pltpu.roll(x, s, axis=0), 0.0)


def _shift_up(x, s, seq_pos, seq_len):
    return jnp.where(seq_pos < seq_len - s, pltpu.roll(x, x.shape[0] - s, axis=0), 0.0)


def _in_even_kernel(*refs, has_pos, seq_len):
    refs, project, seq_pos = _norm_with_halo(refs, has_pos, seq_len, POOL_GW)
    wgate_ref, bgate_ref, pw_ref, ps_ref, qk_ref, v_ref, gate_ref, lf_ref, pool_ref = refs
    p = project(0, EVEN_IN_PAD, with_halo=True)
    qk_ref[:, :GLA_QK] = p[MAIN, :GLA_QK] * (GLA_DK ** -0.5)
    qk_ref[:, GLA_QK:] = p[MAIN, GLA_QK:2 * GLA_QK]
    v_ref[...] = p[MAIN, 512:1024].astype(BF16)
    gate_ref[...] = p[MAIN, 1024:1536]
    pre = _dot(p[MAIN, 2048:].astype(BF16), wgate_ref[...]) + bgate_ref[...]
    log_sig = jnp.minimum(pre, 0.0) - jnp.log(1.0 + jnp.exp(-jnp.abs(pre)))
    lf_ref[...] = log_sig / GATE_TAU

    for gi, w in enumerate(POOL_WINDOWS):
        sl = slice(gi * POOL_GW, (gi + 1) * POOL_GW)
        u = p[:, 1536 + gi * POOL_GW:1536 + (gi + 1) * POOL_GW]
        half = w // 2
        trail, lead, m = u, u, 1
        while m < half:
            trail = trail + _shift_down(trail, m, seq_pos)
            lead = lead + _shift_up(lead, m, seq_pos, seq_len)
            m *= 2
        window = _shift_down(trail, 1, seq_pos) + lead
        cnt = (jnp.minimum(seq_pos + (w - half), seq_len) - jnp.maximum(seq_pos - half, 0)).astype(F32)
        pooled = (window / cnt - u)[MAIN]
        y = _dot(pooled.astype(BF16), pw_ref[gi])
        pool_ref[:, sl] = (y * ps_ref[:, sl]).astype(BF16)


def _in_odd_kernel(*refs, seq_len):
    refs, project, seq_pos = _norm_with_halo(refs, False, seq_len, MIX_D)
    cw_ref, cb_ref, f_ref, conv_ref = refs
    f_ref[...] = project(0, MIX_C).astype(BF16)
    z = project(MIX_C + 2 * MIX_D, MIX_C + 3 * MIX_D, with_halo=True) * project(MIX_C, MIX_C + MIX_D, with_halo=True)
    conv = cb_ref[...] + _shift_down(z, 1, seq_pos)[MAIN] * cw_ref[0:1, :]
    conv = conv + z[MAIN] * cw_ref[1:2, :]
    conv = conv + _shift_up(z, 1, seq_pos, seq_len)[MAIN] * cw_ref[2:3, :]
    conv_ref[...] = (project(MIX_C + MIX_D, MIX_C + 2 * MIX_D) * conv).astype(BF16)


def _mod_spec(layer, seq_len, shared):
    base = layer * COND_ROWS
    if shared:
        return pl.BlockSpec((1, 1, 6 * D_MODEL), lambda i: (base, 0, 0))
    return pl.BlockSpec((1, 1, 6 * D_MODEL), lambda i: (base + 1 + i * TOKEN_TILE // seq_len, 0, 0))


def _tile_spec(cols):
    return pl.BlockSpec((TOKEN_TILE, cols), lambda i: (i, 0))


def _pos_spec(seq_len):
    tiles = seq_len // TOKEN_TILE
    return pl.BlockSpec((TOKEN_TILE, D_MODEL), lambda i: (i % tiles, 0))


def _halo_specs(n_rows, period_tiles):
    per_tile = TOKEN_TILE // HALO
    last = n_rows // HALO - 1
    return [
        pl.BlockSpec((HALO, D_MODEL), lambda i: (jnp.maximum((i % period_tiles) * per_tile - 1, 0), 0)),
        pl.BlockSpec((TOKEN_TILE, D_MODEL), lambda i: (i % period_tiles, 0)),
        pl.BlockSpec((HALO, D_MODEL), lambda i: (jnp.minimum((i % period_tiles + 1) * per_tile, last), 0)),
    ]


def _in_even(x, pos, mods, wp, layer, seq_len, shared):
    n = x.shape[0]
    j = layer // 2
    outs = [(512, F32), (512, BF16), (512, F32), (512, F32), (MIX_B, BF16)]
    has_pos = pos is not None
    in_specs = _halo_specs(n, n // TOKEN_TILE) + (_halo_specs(seq_len, seq_len // TOKEN_TILE) if has_pos else []) + [
        _mod_spec(layer, seq_len, shared), _resident((1, D_MODEL), 2 * layer),
        _resident((D_MODEL, EVEN_IN_PAD), j), _resident((LANES, 2 * GLA_QK), j), _resident((1, 2 * GLA_QK), j),
        _resident((len(POOL_WINDOWS), POOL_GW, POOL_GW), j), _resident((1, MIX_B), j),
    ]
    args = [x] * 3 + ([pos] * 3 if has_pos else []) + [
        mods, wp['norm_g'], wp['w_even'], wp['wgate'], wp['bgate'], wp['pool_w'], wp['pool_s']]
    return pl.pallas_call(
        functools.partial(_in_even_kernel, has_pos=has_pos, seq_len=seq_len),
        grid=(n // TOKEN_TILE,),
        in_specs=in_specs,
        out_specs=[_tile_spec(c) for c, _ in outs],
        out_shape=[jax.ShapeDtypeStruct((n, c), dt) for c, dt in outs],
        compiler_params=_params(1),
        name="in_even",
    )(*args)


def _in_odd(x, mods, wp, layer, seq_len, shared):
    n = x.shape[0]
    j = layer // 2
    outs = [(MIX_C, BF16), (MIX_D, BF16)]
    return pl.pallas_call(
        functools.partial(_in_odd_kernel, seq_len=seq_len),
        grid=(n // TOKEN_TILE,),
        in_specs=_halo_specs(n, n // TOKEN_TILE) + [
            _mod_spec(layer, seq_len, shared), _resident((1, D_MODEL), 2 * layer),
            _resident((D_MODEL, MIX_C + 3 * MIX_D), j),
            _resident(wp['conv_w'].shape[1:], j), _resident((1, MIX_D), j),
        ],
        out_specs=[_tile_spec(c) for c, _ in outs],
        out_shape=[jax.ShapeDtypeStruct((n, c), dt) for c, dt in outs],
        compiler_params=_params(1),
        name="in_odd",
    )(x, x, x, mods, wp['norm_g'], wp['w_odd'], wp['conv_w'], wp['conv_b'])


FF_BLOCK = 1024


def _post_kernel(*refs, has_pos, final):
    x_ref, refs = refs[0], refs[1:]
    pos_ref = None
    if has_pos:
        pos_ref, refs = refs[0], refs[1:]
    a_ref, b_ref, mod_ref, g_ref, wo_ref, w1_ref, w2_ref, fg_ref, out_ref = refs
    half = D_MODEL // 2
    y = _dot(a_ref[...], wo_ref[:half, :]) + _dot(b_ref[...], wo_ref[half:, :])
    gate1 = mod_ref[0, :, 2 * D_MODEL:3 * D_MODEL]
    gate2 = mod_ref[0, :, 5 * D_MODEL:6 * D_MODEL]
    x1 = _load_x(x_ref, pos_ref) + gate1 * y
    h = _modulated_norm(x1, mod_ref, g_ref, 1).astype(BF16)
    acc = jnp.zeros_like(x1)
    for c in range(D_FF // FF_BLOCK):
        a = jnp.maximum(_dot(h, w1_ref[:, c * FF_BLOCK:(c + 1) * FF_BLOCK]), 0.0)
        acc = acc + _dot((a * a).astype(BF16), w2_ref[c * FF_BLOCK:(c + 1) * FF_BLOCK, :])
    x2 = x1 + gate2 * acc
    if final:
        x2 = _rmsnorm(x2, fg_ref[...])
    out_ref[...] = x2


def _post(x, pos, a, b, mods, wp, layer, seq_len, shared):
    n = x.shape[0]
    j = layer // 2
    final = layer == DEPTH - 1
    has_pos = pos is not None
    w_out = wp['w_out_even'] if layer % 2 == 0 else wp['w_out_odd']
    in_specs = [_tile_spec(D_MODEL)] + ([_pos_spec(seq_len)] if has_pos else []) + [
        _tile_spec(D_MODEL // 2), _tile_spec(D_MODEL // 2),
        _mod_spec(layer, seq_len, shared), _resident((1, D_MODEL), 2 * layer + 1),
        _resident((D_MODEL, D_MODEL), j), _resident((D_MODEL, D_FF), layer), _resident((D_FF, D_MODEL), layer),
        _resident((1, D_MODEL)),
    ]
    args = [x] + ([pos] if has_pos else []) + [a, b, mods, wp['norm_g'], w_out, wp['w1'], wp['w2'], wp['final_g']]
    return pl.pallas_call(
        functools.partial(_post_kernel, has_pos=has_pos, final=final),
        grid=(n // TOKEN_TILE,),
        in_specs=in_specs,
        out_specs=_tile_spec(D_MODEL),
        out_shape=jax.ShapeDtypeStruct((n, D_MODEL), F32),
        compiler_params=_params(1),
        name="post_final" if final else "post",
    )(*args)


def _head_block_mask():
    r = lax.broadcasted_iota(jnp.int32, (GLA_QK, MIX_A), 0) >> LOG2_DK
    c = lax.broadcasted_iota(jnp.int32, (GLA_QK, MIX_A), 1) >> LOG2_DV
    return jnp.where(r == c, 1.0, 0.0).astype(F32)


def _gla_chunk(qk_ref, v_ref, lf_ref, state, backward):
    C = GLA_CHUNK
    q = qk_ref[:, :GLA_QK]
    k = qk_ref[:, GLA_QK:]
    v = v_ref[...]
    lf = lf_ref[...]

    row = lax.broadcasted_iota(jnp.int32, (C, C), 0)
    col = lax.broadcasted_iota(jnp.int32, (C, C), 1)
    causal = (col >= row) if backward else (col <= row)
    cum = _split_dot(jnp.where(causal, 1.0, 0.0).astype(BF16), lf)
    total_row = cum[0:1, :] if backward else cum[C - 1:C, :]
    mid = cum[C // 2:C // 2 + 1, :]

    q_mid = q * jnp.exp(cum - mid)
    k_mid = k * jnp.exp(mid - cum)
    o = _dot((q_mid * jnp.exp(mid)).astype(BF16), state.astype(BF16))

    q_c = q_mid.astype(BF16)
    k_c = k_mid.astype(BF16)
    lane_head = lax.broadcasted_iota(jnp.int32, (1, GLA_QK), 1) >> LOG2_DK
    intra = []
    for h in range(GLA_HEADS):
        k_h = k_c * jnp.where(lane_head == h, 1.0, 0.0).astype(BF16)
        att = jnp.where(causal, _dot_nt(q_c, k_h), 0.0).astype(BF16)
        intra.append(_dot(att, v[:, h * GLA_DV:(h + 1) * GLA_DV]))
    o = o + jnp.concatenate(intra, axis=1)

    k_out = (k_mid * jnp.exp(total_row - mid)).T.astype(BF16)
    decay = jnp.exp(jnp.sum(lf.T, axis=1, keepdims=True))
    mask = _head_block_mask()
    new_state = state * decay + _dot(k_out, v) * mask
    return o, new_state


def _gla_kernel(*refs, n_chunks, has_state, want_state):
    (qkf_ref, vf_ref, lff_ref, qkb_ref, vb_ref, lfb_ref, gate_ref, gn_ref), refs = refs[:8], refs[8:]
    s0_ref = None
    if has_state:
        s0_ref, refs = refs[0], refs[1:]
    o_ref, refs = refs[0], refs[1:]
    sout_ref = None
    if want_state:
        sout_ref, refs = refs[0], refs[1:]
    state_ref, part_f_ref, part_b_ref = refs

    C = GLA_CHUNK
    step = pl.program_id(1)

    @pl.when(step == 0)
    def _():
        for d in range(2):
            if has_state:
                state_ref[d] = jnp.concatenate([s0_ref[0, d]] * GLA_HEADS, axis=1) * _head_block_mask()
            else:
                state_ref[d] = jnp.zeros((GLA_QK, MIX_A), F32)

    o_f, state_f = _gla_chunk(qkf_ref, vf_ref, lff_ref, state_ref[0], backward=False)
    o_b, state_b = _gla_chunk(qkb_ref, vb_ref, lfb_ref, state_ref[1], backward=True)
    state_ref[0] = state_f
    state_ref[1] = state_b

    rows_f = pl.ds(pl.multiple_of(step * C, C), C)
    rows_b = pl.ds(pl.multiple_of((n_chunks - 1 - step) * C, C), C)
    part_f_ref[rows_f, :] = o_f
    part_b_ref[rows_b, :] = o_b

    def finish(rows):
        both = part_f_ref[rows, :] + part_b_ref[rows, :]
        normed = []
        for h in range(GLA_HEADS):
            o_h = both[:, h * GLA_DV:(h + 1) * GLA_DV]
            ms = jnp.mean(o_h * o_h, axis=-1, keepdims=True)
            normed.append(o_h * lax.rsqrt(ms + EPS) * gn_ref[...])
        o_ref[rows, :] = (jnp.concatenate(normed, axis=1) * _silu(gate_ref[rows, :])).astype(BF16)

    @pl.when(2 * step >= n_chunks - 1)
    def _():
        finish(rows_f)

    @pl.when(2 * step > n_chunks - 1)
    def _():
        finish(rows_b)

    if want_state:
        @pl.when(step == n_chunks - 1)
        def _():
            for d, st in enumerate((state_f, state_b)):
                s = st[:, :GLA_DV]
                for h in range(1, GLA_HEADS):
                    s = s + st[:, h * GLA_DV:(h + 1) * GLA_DV]
                sout_ref[0, d] = s


def _gla(qk, v, lf, gate, wp, layer, s0, batch, seq_len, want_state):
    n = qk.shape[0]
    n_chunks = seq_len // GLA_CHUNK
    has_state = s0 is not None

    def fwd_map(col):
        return lambda b, s: (b * n_chunks + s, col)

    def bwd_map(col):
        return lambda b, s: (b * n_chunks + n_chunks - 1 - s, col)

    seq_spec = pl.BlockSpec((seq_len, MIX_A), lambda b, s: (b, 0))
    state_spec = pl.BlockSpec((1, 2, GLA_QK, GLA_DV), lambda b, s: (b, 0, 0, 0))
    in_specs = []
    for chunk_map in (fwd_map, bwd_map):
        in_specs += [
            pl.BlockSpec((GLA_CHUNK, 2 * GLA_QK), chunk_map(0)),
            pl.BlockSpec((GLA_CHUNK, MIX_A), chunk_map(0)),
            pl.BlockSpec((GLA_CHUNK, GLA_QK), chunk_map(0 if chunk_map is fwd_map else 1)),
        ]
    in_specs += [seq_spec, _resident((1, GLA_DV), layer // 2)]
    args = [qk, v, lf, qk, v, lf, gate, wp['gn']]
    if has_state:
        in_specs.append(state_spec)
        args.append(s0)
    out_specs = [seq_spec]
    out_shape = [jax.ShapeDtypeStruct((n, MIX_A), BF16)]
    if want_state:
        out_specs.append(state_spec)
        out_shape.append(jax.ShapeDtypeStruct((batch, 2, GLA_QK, GLA_DV), F32))
    outs = pl.pallas_call(
        functools.partial(_gla_kernel, n_chunks=n_chunks, has_state=has_state, want_state=want_state),
        grid=(batch, n_chunks),
        in_specs=in_specs,
        out_specs=out_specs,
        out_shape=out_shape,
        scratch_shapes=[pltpu.VMEM((2, GLA_QK, MIX_A), F32), pltpu.VMEM((seq_len, MIX_A), F32),
                        pltpu.VMEM((seq_len, MIX_A), F32)],
        compiler_params=_params(2),
        name="gla",
    )(*args)
    return (outs[0], outs[1]) if want_state else (outs[0], None)


DFT_ROWS = 512
DFT_LOW = 32


def _dft_rows(rows, n):
    k = jnp.arange(n, dtype=jnp.int32)
    ang = ((rows[:, None] * k[None, :]) % n).astype(F32) * (2.0 * math.pi / n)
    return jnp.cos(ang), jnp.sin(ang)


def _dft_tables(n):
    c_lo, s_lo = _dft_rows(jnp.arange(DFT_LOW, dtype=jnp.int32), n)
    c_hi, s_hi = _dft_rows(jnp.arange(n // DFT_LOW, dtype=jnp.int32) * DFT_LOW, n)
    cos = c_hi[:, None, :] * c_lo[None, :, :] - s_hi[:, None, :] * s_lo[None, :, :]
    sin = s_hi[:, None, :] * c_lo[None, :, :] + c_hi[:, None, :] * s_lo[None, :, :]
    return cos.reshape(n, n), sin.reshape(n, n)


def _fourier_kernel(f_ref, cs_ref, cl_ref, sl_ref, out_ref, xc_ref, xs_ref, *, scale):
    n = f_ref.shape[0]
    for g in range(MIX_C // FOURIER_GW):
        sl = slice(g * FOURIER_GW, (g + 1) * FOURIER_GW)
        x = _dot(f_ref[:, sl], cs_ref[...])
        xc_ref[:, sl] = x[:, :FOURIER_GW].astype(BF16)
        xs_ref[:, sl] = x[:, FOURIER_GW:].astype(BF16)

    block = min(DFT_ROWS, n)

    def body(r, carry):
        rows = pl.ds(pl.multiple_of(r * block, block), block)
        re = _dot(cl_ref[rows, :], xc_ref[...]) - _dot(sl_ref[rows, :], xs_ref[...])
        out_ref[rows, :] = (re * scale).astype(BF16)
        return carry

    lax.fori_loop(0, n // block, body, 0)


def _fourier_tables(seq_len):
    cc, sc = _dft_rows(jnp.arange(FOURIER_GW, dtype=jnp.int32), FOURIER_GW)
    cl, sl = _dft_tables(seq_len)
    return jnp.concatenate([cc, sc], axis=1).astype(BF16), cl.astype(BF16), sl.astype(BF16)


def _fourier(f, tables, batch, seq_len):
    n = f.shape[0]
    scale = 1.0 / math.sqrt(seq_len * FOURIER_GW)
    seq_spec = pl.BlockSpec((seq_len, MIX_C), lambda b: (b, 0))
    return pl.pallas_call(
        functools.partial(_fourier_kernel, scale=scale),
        grid=(batch,),
        in_specs=[seq_spec, _resident((FOURIER_GW, 2 * FOURIER_GW)),
                  _resident((seq_len, seq_len)), _resident((seq_len, seq_len))],
        out_specs=seq_spec,
        out_shape=jax.ShapeDtypeStruct((n, MIX_C), BF16),
        scratch_shapes=[pltpu.VMEM((seq_len, MIX_C), BF16), pltpu.VMEM((seq_len, MIX_C), BF16)],
        compiler_params=_params(1),
        name="fourier",
    )(f, *tables)


def _grid_pos_embed(n_tok):
    rows = n_tok // GRID_W
    quarter = D_MODEL // 4
    freqs = 1.0 / (10000.0 ** (jnp.arange(quarter, dtype=F32) / quarter))
    ar = jnp.arange(rows, dtype=F32)[:, None] * freqs
    ac = jnp.arange(GRID_W, dtype=F32)[:, None] * freqs

    def per_row(t):
        return jnp.broadcast_to(t[:, None, :], (rows, GRID_W, quarter))

    def per_col(t):
        return jnp.broadcast_to(t[None, :, :], (rows, GRID_W, quarter))

    pos = jnp.concatenate([per_row(jnp.sin(ar)), per_row(jnp.cos(ar)), per_col(jnp.sin(ac)), per_col(jnp.cos(ac))],
                          axis=-1)
    return pos.reshape(n_tok, D_MODEL)


def _prep_weights(norm_g, final_g, w_mlp1, w_mlp2, w_in_even, w_a2, b_a2, gla_norm_g, pool_w, pool_s,
                  w_out_even, w_in_odd, conv_w, conv_b, w_out_odd):
    o4 = 2 * GLA_QK + 2 * MIX_A
    o5 = o4 + 2 * GATE_RANK
    w_in_even = w_in_even.astype(BF16)
    w_even = jnp.concatenate(
        [w_in_even[:, :, :o4], w_in_even[:, :, o5:], w_in_even[:, :, o4:o5],
         jnp.zeros((N_EVEN, D_MODEL, LANES - 2 * GATE_RANK), BF16)], axis=-1)
    wgate = jnp.zeros((N_EVEN, LANES, 2 * GLA_QK), F32)
    wgate = wgate.at[:, :GATE_RANK, :GLA_QK].set(w_a2[:, 0])
    wgate = wgate.at[:, GATE_RANK:2 * GATE_RANK, GLA_QK:].set(w_a2[:, 1])
    return dict(
        norm_g=norm_g.reshape(DEPTH * 2, 1, D_MODEL),
        final_g=final_g.reshape(1, D_MODEL),
        w1=w_mlp1.astype(BF16), w2=w_mlp2.astype(BF16),
        w_even=w_even, wgate=wgate.astype(BF16), bgate=b_a2.reshape(N_EVEN, 1, 2 * GLA_QK),
        gn=gla_norm_g.reshape(N_EVEN, 1, GLA_DV),
        pool_w=pool_w.astype(BF16), pool_s=pool_s.reshape(N_EVEN, 1, MIX_B),
        w_out_even=w_out_even.astype(BF16),
        w_odd=w_in_odd.astype(BF16), conv_w=conv_w, conv_b=conv_b.reshape(-1, 1, MIX_D),
        w_out_odd=w_out_odd.astype(BF16),
    )


def _trunk(x, pos, mods, gla_init, shared_mod, want_state, wp):
    batch, seq_len, _ = x.shape
    x = x.reshape(batch * seq_len, D_MODEL)
    tables = _fourier_tables(seq_len)
    states = []
    for l in range(DEPTH):
        j = l // 2
        if l % 2 == 0:
            qk, v, gate, lf, b = _in_even(x, pos, mods, wp, l, seq_len, shared_mod)
            s0 = None if gla_init is None else gla_init[:, j].reshape(batch, 2, GLA_QK, GLA_DV)
            a, s = _gla(qk, v, lf, gate, wp, l, s0, batch, seq_len, want_state)
            if want_state:
                states.append(s.reshape(batch, 2, GLA_HEADS, GLA_DK, GLA_DV))
        else:
            f, b = _in_odd(x, mods, wp, l, seq_len, shared_mod)
            a = _fourier(f, tables, batch, seq_len)
        x = _post(x, pos, a, b, mods, wp, l, seq_len, shared_mod)
        pos = None
    y = x.reshape(batch, seq_len, D_MODEL)
    return y, (jnp.stack(states, axis=1) if want_state else None)


def kernel(x_prompt, x_sample, state_gla, c, c_ctx, norm_g, final_g, w_ada, b_ada, w_mlp1, w_mlp2,
           w_in_even, w_a2, b_a2, gla_norm_g, pool_w, pool_s, w_out_even,
           w_in_odd, conv_w, conv_b, w_out_odd):
    wp = _prep_weights(norm_g, final_g, w_mlp1, w_mlp2, w_in_even, w_a2, b_a2, gla_norm_g, pool_w, pool_s,
                       w_out_even, w_in_odd, conv_w, conv_b, w_out_odd)
    n_dec = c.shape[0]
    cond = jnp.concatenate([c_ctx[None, :], c, jnp.zeros((COND_ROWS - 1 - n_dec, D_MODEL), F32)], axis=0)
    mods = _ada(cond, w_ada, b_ada).reshape(DEPTH * COND_ROWS, 1, 6 * D_MODEL)

    y_prompt, new_state = _trunk(x_prompt, None, mods, None, True, True, wp)
    pos = _grid_pos_embed(x_sample.shape[1]).astype(x_sample.dtype)
    y_sample, _ = _trunk(x_sample, pos, mods, state_gla, False, False, wp)
    return (y_prompt, y_sample, new_state.astype(x_prompt.dtype))
```

```python
import functools
import math

import jax
import jax.numpy as jnp
from jax import lax
from jax.experimental import pallas as pl
from jax.experimental.pallas import tpu as pltpu

D_MODEL = 1024
DEPTH = 4
N_EVEN = 2
GRID_W = 64
MIX_A = 512
MIX_B = 512
GLA_HEADS = 4
GLA_DV = 128
GLA_DK = 64
GLA_QK = 256
LOG2_DK = 6
LOG2_DV = 7
GATE_RANK = 16
GATE_TAU = 16.0
POOL_WINDOWS = (2, 4, 8, 16)
POOL_GW = 128
MIX_C = 512
MIX_D = 512
FOURIER_GW = 128
D_FF = 4096
EPS = 1e-6

LANES = 128
SUBLANES = 8
TOKEN_TILE = 512
GLA_CHUNK = 256
EVEN_MAIN = 2 * GLA_QK + 2 * MIX_A
VMEM_LIMIT = 56 * 1024 * 1024

F32 = jnp.float32
BF16 = jnp.bfloat16


def _dot(a, b):
    return jnp.dot(a, b, preferred_element_type=F32)


def _dot_nt(a, b):
    return lax.dot_general(a, b, (((1,), (1,)), ((), ())), preferred_element_type=F32)


def _split_dot(m, x):
    hi = x.astype(BF16)
    lo = (x - hi.astype(F32)).astype(BF16)
    return _dot(m, hi) + _dot(m, lo)


def _silu(x):
    return x * (1.0 / (1.0 + jnp.exp(-x)))


def _rmsnorm(x, g):
    ms = jnp.mean(x * x, axis=-1, keepdims=True)
    return (x * lax.rsqrt(ms + EPS)) * g


def _params(n_axes):
    return pltpu.CompilerParams(dimension_semantics=("arbitrary",) * n_axes, vmem_limit_bytes=VMEM_LIMIT)


def _resident(shape, layer=None):
    zeros = (0,) * len(shape)
    if layer is None:
        return pl.BlockSpec(shape, lambda *_: zeros, pipeline_mode=pl.Buffered(1))
    return pl.BlockSpec((None,) + tuple(shape), lambda *_: (layer,) + zeros, pipeline_mode=pl.Buffered(1))


ADA_COLS = 1536
COND_ROWS = SUBLANES


def _ada_kernel(cond_ref, w_ref, b_ref, out_ref):
    s = _silu(cond_ref[...]).astype(BF16)
    out_ref[0] = _dot(s, w_ref[0].astype(BF16)) + b_ref[0]


def _ada(cond, w_ada, b_ada):
    n_out = w_ada.shape[-1]
    return pl.pallas_call(
        _ada_kernel,
        grid=(DEPTH, n_out // ADA_COLS),
        in_specs=[
            pl.BlockSpec((COND_ROWS, D_MODEL), lambda l, j: (0, 0)),
            pl.BlockSpec((1, D_MODEL, ADA_COLS), lambda l, j: (l, 0, j)),
            pl.BlockSpec((1, 1, ADA_COLS), lambda l, j: (l, 0, j)),
        ],
        out_specs=pl.BlockSpec((1, COND_ROWS, ADA_COLS), lambda l, j: (l, 0, j)),
        out_shape=jax.ShapeDtypeStruct((DEPTH, COND_ROWS, n_out), F32),
        compiler_params=_params(2),
        name="ada_mod",
    )(cond, w_ada, b_ada.reshape(DEPTH, 1, n_out))


def _modulated_norm(x, mod_ref, g_ref, which):
    shift = mod_ref[0, :, (3 * which) * D_MODEL:(3 * which + 1) * D_MODEL]
    scale = mod_ref[0, :, (3 * which + 1) * D_MODEL:(3 * which + 2) * D_MODEL]
    return _rmsnorm(x, g_ref[...]) * (1.0 + scale) + shift


def _load_x(x_ref, pos_ref):
    return x_ref[...] if pos_ref is None else x_ref[...] + pos_ref[...]


HALO = SUBLANES
HALO_TILE = TOKEN_TILE + 2 * HALO
MAIN = slice(HALO, HALO + TOKEN_TILE)


def _norm_with_halo(refs, has_pos, seq_len, pos_cols):
    x_refs, refs = refs[:3], refs[3:]
    x = jnp.concatenate([r[...] for r in x_refs], axis=0)
    if has_pos:
        pos_refs, refs = refs[:3], refs[3:]
        x = x + jnp.concatenate([r[...] for r in pos_refs], axis=0)
    mod_ref, g_ref, refs = refs[0], refs[1], refs[2:]
    h = _modulated_norm(x, mod_ref, g_ref, 0).astype(BF16)
    row = lax.broadcasted_iota(jnp.int32, (HALO_TILE, pos_cols), 0)
    seq_pos = (pl.program_id(0) * TOKEN_TILE - HALO + row) & (seq_len - 1)
    return refs, h, seq_pos


def _shift_down(x, s, seq_pos):
    return jnp.where(seq_pos >= s, pltpu.roll(x, s, axis=0), 0.0)


def _shift_up(x, s, seq_pos, seq_len):
    return jnp.where(seq_pos < seq_len - s, pltpu.roll(x, x.shape[0] - s, axis=0), 0.0)


def _in_even_kernel(*refs, has_pos, seq_len):
    refs, h, seq_pos = _norm_with_halo(refs, has_pos, seq_len, POOL_GW)
    (w_ref, wu_ref, wr_ref, wgate_ref, bgate_ref, pw_ref, ps_ref,
     qk_ref, v_ref, gate_ref, lf_ref, pool_ref) = refs
    h_main = h[MAIN]
    p = _dot(h_main, w_ref[...])
    qk_ref[:, :GLA_QK] = p[:, :GLA_QK] * (GLA_DK ** -0.5)
    qk_ref[:, GLA_QK:] = p[:, GLA_QK:2 * GLA_QK]
    v_ref[...] = p[:, 512:1024].astype(BF16)
    gate_ref[...] = p[:, 1024:1536]
    pre = _dot(_dot(h_main, wr_ref[...]).astype(BF16), wgate_ref[...]) + bgate_ref[...]
    log_sig = jnp.minimum(pre, 0.0) - jnp.log(1.0 + jnp.exp(-jnp.abs(pre)))
    lf_ref[...] = log_sig / GATE_TAU

    u_all = _dot(h, wu_ref[...])
    for gi, w in enumerate(POOL_WINDOWS):
        sl = slice(gi * POOL_GW, (gi + 1) * POOL_GW)
        u = u_all[:, sl]
        half = w // 2
        trail, lead, m = u, u, 1
        while m < half:
            trail = trail + _shift_down(trail, m, seq_pos)
            lead = lead + _shift_up(lead, m, seq_pos, seq_len)
            m *= 2
        window = _shift_down(trail, 1, seq_pos) + lead
        cnt = (jnp.minimum(seq_pos + (w - half), seq_len) - jnp.maximum(seq_pos - half, 0)).astype(F32)
        pooled = (window / cnt - u)[MAIN]
        y = _dot(pooled.astype(BF16), pw_ref[gi])
        pool_ref[:, sl] = (y * ps_ref[:, sl]).astype(BF16)


def _in_odd_kernel(*refs, seq_len):
    refs, h, seq_pos = _norm_with_halo(refs, False, seq_len, MIX_D)
    w_ref, cw_ref, cb_ref, f_ref, conv_ref = refs
    h_main = h[MAIN]
    f_ref[...] = _dot(h_main, w_ref[:, :MIX_C]).astype(BF16)
    z = _dot(h, w_ref[:, MIX_C + 2 * MIX_D:]) * _dot(h, w_ref[:, MIX_C:MIX_C + MIX_D])
    conv = cb_ref[...] + _shift_down(z, 1, seq_pos)[MAIN] * cw_ref[0:1, :]
    conv = conv + z[MAIN] * cw_ref[1:2, :]
    conv = conv + _shift_up(z, 1, seq_pos, seq_len)[MAIN] * cw_ref[2:3, :]
    conv_ref[...] = (_dot(h_main, w_ref[:, MIX_C + MIX_D:MIX_C + 2 * MIX_D]) * conv).astype(BF16)


def _mod_spec(layer, seq_len, shared):
    base = layer * COND_ROWS
    if shared:
        return pl.BlockSpec((1, 1, 6 * D_MODEL), lambda i: (base, 0, 0))
    return pl.BlockSpec((1, 1, 6 * D_MODEL), lambda i: (base + 1 + i * TOKEN_TILE // seq_len, 0, 0))


def _tile_spec(cols):
    return pl.BlockSpec((TOKEN_TILE, cols), lambda i: (i, 0))


def _pos_spec(seq_len):
    tiles = seq_len // TOKEN_TILE
    return pl.BlockSpec((TOKEN_TILE, D_MODEL), lambda i: (i % tiles, 0))


def _halo_specs(n_rows, period_tiles):
    per_tile = TOKEN_TILE // HALO
    last = n_rows // HALO - 1
    return [
        pl.BlockSpec((HALO, D_MODEL), lambda i: (jnp.maximum((i % period_tiles) * per_tile - 1, 0), 0)),
        pl.BlockSpec((TOKEN_TILE, D_MODEL), lambda i: (i % period_tiles, 0)),
        pl.BlockSpec((HALO, D_MODEL), lambda i: (jnp.minimum((i % period_tiles + 1) * per_tile, last), 0)),
    ]


def _in_even(x, pos, mods, wp, layer, seq_len, shared):
    n = x.shape[0]
    j = layer // 2
    outs = [(512, F32), (512, BF16), (512, F32), (512, F32), (MIX_B, BF16)]
    has_pos = pos is not None
    in_specs = _halo_specs(n, n // TOKEN_TILE) + (_halo_specs(seq_len, seq_len // TOKEN_TILE) if has_pos else []) + [
        _mod_spec(layer, seq_len, shared), _resident((1, D_MODEL), 2 * layer),
        _resident((D_MODEL, EVEN_MAIN), j), _resident((D_MODEL, MIX_B), j), _resident((D_MODEL, LANES), j),
        _resident((LANES, 2 * GLA_QK), j), _resident((1, 2 * GLA_QK), j),
        _resident((len(POOL_WINDOWS), POOL_GW, POOL_GW), j), _resident((1, MIX_B), j),
    ]
    args = [x] * 3 + ([pos] * 3 if has_pos else []) + [
        mods, wp['norm_g'], wp['w_even'], wp['w_pool_in'], wp['w_rank'], wp['wgate'], wp['bgate'],
        wp['pool_w'], wp['pool_s']]
    return pl.pallas_call(
        functools.partial(_in_even_kernel, has_pos=has_pos, seq_len=seq_len),
        grid=(n // TOKEN_TILE,),
        in_specs=in_specs,
        out_specs=[_tile_spec(c) for c, _ in outs],
        out_shape=[jax.ShapeDtypeStruct((n, c), dt) for c, dt in outs],
        compiler_params=_params(1),
        name="in_even",
    )(*args)


def _in_odd(x, mods, wp, layer, seq_len, shared):
    n = x.shape[0]
    j = layer // 2
    outs = [(MIX_C, BF16), (MIX_D, BF16)]
    return pl.pallas_call(
        functools.partial(_in_odd_kernel, seq_len=seq_len),
        grid=(n // TOKEN_TILE,),
        in_specs=_halo_specs(n, n // TOKEN_TILE) + [
            _mod_spec(layer, seq_len, shared), _resident((1, D_MODEL), 2 * layer),
            _resident((D_MODEL, MIX_C + 3 * MIX_D), j),
            _resident(wp['conv_w'].shape[1:], j), _resident((1, MIX_D), j),
        ],
        out_specs=[_tile_spec(c) for c, _ in outs],
        out_shape=[jax.ShapeDtypeStruct((n, c), dt) for c, dt in outs],
        compiler_params=_params(1),
        name="in_odd",
    )(x, x, x, mods, wp['norm_g'], wp['w_odd'], wp['conv_w'], wp['conv_b'])


FF_BLOCK = 1024


def _post_kernel(*refs, has_pos, final):
    x_ref, refs = refs[0], refs[1:]
    pos_ref = None
    if has_pos:
        pos_ref, refs = refs[0], refs[1:]
    a_ref, b_ref, mod_ref, g_ref, wo_ref, w1_ref, w2_ref, fg_ref, out_ref = refs
    half = D_MODEL // 2
    y = _dot(a_ref[...], wo_ref[:half, :]) + _dot(b_ref[...], wo_ref[half:, :])
    gate1 = mod_ref[0, :, 2 * D_MODEL:3 * D_MODEL]
    gate2 = mod_ref[0, :, 5 * D_MODEL:6 * D_MODEL]
    x1 = _load_x(x_ref, pos_ref) + gate1 * y
    h = _modulated_norm(x1, mod_ref, g_ref, 1).astype(BF16)
    acc = jnp.zeros_like(x1)
    for c in range(D_FF // FF_BLOCK):
        a = jnp.maximum(_dot(h, w1_ref[:, c * FF_BLOCK:(c + 1) * FF_BLOCK]), 0.0)
        acc = acc + _dot((a * a).astype(BF16), w2_ref[c * FF_BLOCK:(c + 1) * FF_BLOCK, :])
    x2 = x1 + gate2 * acc
    if final:
        x2 = _rmsnorm(x2, fg_ref[...])
    out_ref[...] = x2


def _post(x, pos, a, b, mods, wp, layer, seq_len, shared):
    n = x.shape[0]
    j = layer // 2
    final = layer == DEPTH - 1
    has_pos = pos is not None
    w_out = wp['w_out_even'] if layer % 2 == 0 else wp['w_out_odd']
    in_specs = [_tile_spec(D_MODEL)] + ([_pos_spec(seq_len)] if has_pos else []) + [
        _tile_spec(D_MODEL // 2), _tile_spec(D_MODEL // 2),
        _mod_spec(layer, seq_len, shared), _resident((1, D_MODEL), 2 * layer + 1),
        _resident((D_MODEL, D_MODEL), j), _resident((D_MODEL, D_FF), layer), _resident((D_FF, D_MODEL), layer),
        _resident((1, D_MODEL)),
    ]
    args = [x] + ([pos] if has_pos else []) + [a, b, mods, wp['norm_g'], w_out, wp['w1'], wp['w2'], wp['final_g']]
    return pl.pallas_call(
        functools.partial(_post_kernel, has_pos=has_pos, final=final),
        grid=(n // TOKEN_TILE,),
        in_specs=in_specs,
        out_specs=_tile_spec(D_MODEL),
        out_shape=jax.ShapeDtypeStruct((n, D_MODEL), F32),
        compiler_params=_params(1),
        name="post_final" if final else "post",
    )(*args)


def _head_block_mask():
    r = lax.broadcasted_iota(jnp.int32, (GLA_QK, MIX_A), 0) >> LOG2_DK
    c = lax.broadcasted_iota(jnp.int32, (GLA_QK, MIX_A), 1) >> LOG2_DV
    return jnp.where(r == c, 1.0, 0.0).astype(F32)


def _gla_chunk(qk_ref, v_ref, lf_ref, state, backward):
    C = GLA_CHUNK
    q = qk_ref[:, :GLA_QK]
    k = qk_ref[:, GLA_QK:]
    v = v_ref[...]
    lf = lf_ref[...]

    row = lax.broadcasted_iota(jnp.int32, (C, C), 0)
    col = lax.broadcasted_iota(jnp.int32, (C, C), 1)
    causal = (col >= row) if backward else (col <= row)
    cum = _split_dot(jnp.where(causal, 1.0, 0.0).astype(BF16), lf)
    total_row = cum[0:1, :] if backward else cum[C - 1:C, :]
    mid = cum[C // 2:C // 2 + 1, :]

    q_mid = q * jnp.exp(cum - mid)
    k_mid = k * jnp.exp(mid - cum)
    o = _dot((q_mid * jnp.exp(mid)).astype(BF16), state.astype(BF16))

    q_c = q_mid.astype(BF16)
    k_c = k_mid.astype(BF16)
    lane_head = lax.broadcasted_iota(jnp.int32, (1, GLA_QK), 1) >> LOG2_DK
    intra = []
    for h in range(GLA_HEADS):
        k_h = k_c * jnp.where(lane_head == h, 1.0, 0.0).astype(BF16)
        att = jnp.where(causal, _dot_nt(q_c, k_h), 0.0).astype(BF16)
        intra.append(_dot(att, v[:, h * GLA_DV:(h + 1) * GLA_DV]))
    o = o + jnp.concatenate(intra, axis=1)

    k_out = (k_mid * jnp.exp(total_row - mid)).T.astype(BF16)
    decay = jnp.exp(jnp.sum(lf.T, axis=1, keepdims=True))
    mask = _head_block_mask()
    new_state = state * decay + _dot(k_out, v) * mask
    return o, new_state


def _gla_kernel(*refs, n_chunks, has_state, want_state):
    (qkf_ref, vf_ref, lff_ref, qkb_ref, vb_ref, lfb_ref, gate_ref, gn_ref), refs = refs[:8], refs[8:]
    s0_ref = None
    if has_state:
        s0_ref, refs = refs[0], refs[1:]
    o_ref, refs = refs[0], refs[1:]
    sout_ref = None
    if want_state:
        sout_ref, refs = refs[0], refs[1:]
    state_ref, part_f_ref, part_b_ref = refs

    C = GLA_CHUNK
    step = pl.program_id(1)

    @pl.when(step == 0)
    def _():
        for d in range(2):
            if has_state:
                state_ref[d] = jnp.concatenate([s0_ref[0, d]] * GLA_HEADS, axis=1) * _head_block_mask()
            else:
                state_ref[d] = jnp.zeros((GLA_QK, MIX_A), F32)

    o_f, state_f = _gla_chunk(qkf_ref, vf_ref, lff_ref, state_ref[0], backward=False)
    o_b, state_b = _gla_chunk(qkb_ref, vb_ref, lfb_ref, state_ref[1], backward=True)
    state_ref[0] = state_f
    state_ref[1] = state_b

    rows_f = pl.ds(pl.multiple_of(step * C, C), C)
    rows_b = pl.ds(pl.multiple_of((n_chunks - 1 - step) * C, C), C)
    part_f_ref[rows_f, :] = o_f
    part_b_ref[rows_b, :] = o_b

    def finish(rows):
        both = part_f_ref[rows, :] + part_b_ref[rows, :]
        normed = []
        for h in range(GLA_HEADS):
            o_h = both[:, h * GLA_DV:(h + 1) * GLA_DV]
            ms = jnp.mean(o_h * o_h, axis=-1, keepdims=True)
            normed.append(o_h * lax.rsqrt(ms + EPS) * gn_ref[...])
        o_ref[rows, :] = (jnp.concatenate(normed, axis=1) * _silu(gate_ref[rows, :])).astype(BF16)

    @pl.when(2 * step >= n_chunks - 1)
    def _():
        finish(rows_f)

    @pl.when(2 * step > n_chunks - 1)
    def _():
        finish(rows_b)

    if want_state:
        @pl.when(step == n_chunks - 1)
        def _():
            for d, st in enumerate((state_f, state_b)):
                s = st[:, :GLA_DV]
                for h in range(1, GLA_HEADS):
                    s = s + st[:, h * GLA_DV:(h + 1) * GLA_DV]
                sout_ref[0, d] = s


def _gla(qk, v, lf, gate, wp, layer, s0, batch, seq_len, want_state):
    n = qk.shape[0]
    n_chunks = seq_len // GLA_CHUNK
    has_state = s0 is not None

    def fwd_map(col):
        return lambda b, s: (b * n_chunks + s, col)

    def bwd_map(col):
        return lambda b, s: (b * n_chunks + n_chunks - 1 - s, col)

    seq_spec = pl.BlockSpec((seq_len, MIX_A), lambda b, s: (b, 0))
    state_spec = pl.BlockSpec((1, 2, GLA_QK, GLA_DV), lambda b, s: (b, 0, 0, 0))
    in_specs = []
    for chunk_map in (fwd_map, bwd_map):
        in_specs += [
            pl.BlockSpec((GLA_CHUNK, 2 * GLA_QK), chunk_map(0)),
            pl.BlockSpec((GLA_CHUNK, MIX_A), chunk_map(0)),
            pl.BlockSpec((GLA_CHUNK, GLA_QK), chunk_map(0 if chunk_map is fwd_map else 1)),
        ]
    in_specs += [seq_spec, _resident((1, GLA_DV), layer // 2)]
    args = [qk, v, lf, qk, v, lf, gate, wp['gn']]
    if has_state:
        in_specs.append(state_spec)
        args.append(s0)
    out_specs = [seq_spec]
    out_shape = [jax.ShapeDtypeStruct((n, MIX_A), BF16)]
    if want_state:
        out_specs.append(state_spec)
        out_shape.append(jax.ShapeDtypeStruct((batch, 2, GLA_QK, GLA_DV), F32))
    outs = pl.pallas_call(
        functools.partial(_gla_kernel, n_chunks=n_chunks, has_state=has_state, want_state=want_state),
        grid=(batch, n_chunks),
        in_specs=in_specs,
        out_specs=out_specs,
        out_shape=out_shape,
        scratch_shapes=[pltpu.VMEM((2, GLA_QK, MIX_A), F32), pltpu.VMEM((seq_len, MIX_A), F32),
                        pltpu.VMEM((seq_len, MIX_A), F32)],
        compiler_params=_params(2),
        name="gla",
    )(*args)
    return (outs[0], outs[1]) if want_state else (outs[0], None)


DFT_ROWS = 512
DFT_LOW = 32


def _dft_rows(rows, n):
    k = jnp.arange(n, dtype=jnp.int32)
    ang = ((rows[:, None] * k[None, :]) % n).astype(F32) * (2.0 * math.pi / n)
    return jnp.cos(ang), jnp.sin(ang)


def _dft_tables(n):
    c_lo, s_lo = _dft_rows(jnp.arange(DFT_LOW, dtype=jnp.int32), n)
    c_hi, s_hi = _dft_rows(jnp.arange(n // DFT_LOW, dtype=jnp.int32) * DFT_LOW, n)
    cos = c_hi[:, None, :] * c_lo[None, :, :] - s_hi[:, None, :] * s_lo[None, :, :]
    sin = s_hi[:, None, :] * c_lo[None, :, :] + c_hi[:, None, :] * s_lo[None, :, :]
    return cos.reshape(n, n), sin.reshape(n, n)


def _fourier_kernel(f_ref, cs_ref, cl_ref, sl_ref, out_ref, xc_ref, xs_ref, *, scale):
    n = f_ref.shape[0]
    for g in range(MIX_C // FOURIER_GW):
        sl = slice(g * FOURIER_GW, (g + 1) * FOURIER_GW)
        x = _dot(f_ref[:, sl], cs_ref[...])
        xc_ref[:, sl] = x[:, :FOURIER_GW].astype(BF16)
        xs_ref[:, sl] = x[:, FOURIER_GW:].astype(BF16)

    block = min(DFT_ROWS, n)

    def body(r, carry):
        rows = pl.ds(pl.multiple_of(r * block, block), block)
        re = _dot(cl_ref[rows, :], xc_ref[...]) - _dot(sl_ref[rows, :], xs_ref[...])
        out_ref[rows, :] = (re * scale).astype(BF16)
        return carry

    lax.fori_loop(0, n // block, body, 0)


def _fourier_tables(seq_len):
    cc, sc = _dft_rows(jnp.arange(FOURIER_GW, dtype=jnp.int32), FOURIER_GW)
    cl, sl = _dft_tables(seq_len)
    return jnp.concatenate([cc, sc], axis=1).astype(BF16), cl.astype(BF16), sl.astype(BF16)


def _fourier(f, tables, batch, seq_len):
    n = f.shape[0]
    scale = 1.0 / math.sqrt(seq_len * FOURIER_GW)
    seq_spec = pl.BlockSpec((seq_len, MIX_C), lambda b: (b, 0))
    return pl.pallas_call(
        functools.partial(_fourier_kernel, scale=scale),
        grid=(batch,),
        in_specs=[seq_spec, _resident((FOURIER_GW, 2 * FOURIER_GW)),
                  _resident((seq_len, seq_len)), _resident((seq_len, seq_len))],
        out_specs=seq_spec,
        out_shape=jax.ShapeDtypeStruct((n, MIX_C), BF16),
        scratch_shapes=[pltpu.VMEM((seq_len, MIX_C), BF16), pltpu.VMEM((seq_len, MIX_C), BF16)],
        compiler_params=_params(1),
        name="fourier",
    )(f, *tables)


def _grid_pos_embed(n_tok):
    rows = n_tok // GRID_W
    quarter = D_MODEL // 4
    freqs = 1.0 / (10000.0 ** (jnp.arange(quarter, dtype=F32) / quarter))
    ar = jnp.arange(rows, dtype=F32)[:, None] * freqs
    ac = jnp.arange(GRID_W, dtype=F32)[:, None] * freqs

    def per_row(t):
        return jnp.broadcast_to(t[:, None, :], (rows, GRID_W, quarter))

    def per_col(t):
        return jnp.broadcast_to(t[None, :, :], (rows, GRID_W, quarter))

    pos = jnp.concatenate([per_row(jnp.sin(ar)), per_row(jnp.cos(ar)), per_col(jnp.sin(ac)), per_col(jnp.cos(ac))],
                          axis=-1)
    return pos.reshape(n_tok, D_MODEL)


def _prep_weights(norm_g, final_g, w_mlp1, w_mlp2, w_in_even, w_a2, b_a2, gla_norm_g, pool_w, pool_s,
                  w_out_even, w_in_odd, conv_w, conv_b, w_out_odd):
    rank_end = EVEN_MAIN + 2 * GATE_RANK
    w_even = w_in_even[:, :, :EVEN_MAIN].astype(BF16)
    w_pool_in = w_in_even[:, :, rank_end:].astype(BF16)
    w_rank = jnp.pad(w_in_even[:, :, EVEN_MAIN:rank_end].astype(BF16),
                     ((0, 0), (0, 0), (0, LANES - 2 * GATE_RANK)))
    wgate = jnp.zeros((N_EVEN, LANES, 2 * GLA_QK), F32)
    wgate = wgate.at[:, :GATE_RANK, :GLA_QK].set(w_a2[:, 0])
    wgate = wgate.at[:, GATE_RANK:2 * GATE_RANK, GLA_QK:].set(w_a2[:, 1])
    return dict(
        norm_g=norm_g.reshape(DEPTH * 2, 1, D_MODEL),
        final_g=final_g.reshape(1, D_MODEL),
        w1=w_mlp1.astype(BF16), w2=w_mlp2.astype(BF16),
        w_even=w_even, w_pool_in=w_pool_in, w_rank=w_rank, wgate=wgate.astype(BF16), bgate=b_a2.reshape(N_EVEN, 1, 2 * GLA_QK),
        gn=gla_norm_g.reshape(N_EVEN, 1, GLA_DV),
        pool_w=pool_w.astype(BF16), pool_s=pool_s.reshape(N_EVEN, 1, MIX_B),
        w_out_even=w_out_even.astype(BF16),
        w_odd=w_in_odd.astype(BF16), conv_w=conv_w, conv_b=conv_b.reshape(-1, 1, MIX_D),
        w_out_odd=w_out_odd.astype(BF16),
    )


def _trunk(x, pos, mods, gla_init, shared_mod, want_state, wp):
    batch, seq_len, _ = x.shape
    x = x.reshape(batch * seq_len, D_MODEL)
    tables = _fourier_tables(seq_len)
    states = []
    for l in range(DEPTH):
        j = l // 2
        if l % 2 == 0:
            qk, v, gate, lf, b = _in_even(x, pos, mods, wp, l, seq_len, shared_mod)
            s0 = None if gla_init is None else gla_init[:, j].reshape(batch, 2, GLA_QK, GLA_DV)
            a, s = _gla(qk, v, lf, gate, wp, l, s0, batch, seq_len, want_state)
            if want_state:
                states.append(s.reshape(batch, 2, GLA_HEADS, GLA_DK, GLA_DV))
        else:
            f, b = _in_odd(x, mods, wp, l, seq_len, shared_mod)
            a = _fourier(f, tables, batch, seq_len)
        x = _post(x, pos, a, b, mods, wp, l, seq_len, shared_mod)
        pos = None
    y = x.reshape(batch, seq_len, D_MODEL)
    return y, (jnp.stack(states, axis=1) if want_state else None)


def kernel(x_prompt, x_sample, state_gla, c, c_ctx, norm_g, final_g, w_ada, b_ada, w_mlp1, w_mlp2,
           w_in_even, w_a2, b_a2, gla_norm_g, pool_w, pool_s, w_out_even,
           w_in_odd, conv_w, conv_b, w_out_odd):
    wp = _prep_weights(norm_g, final_g, w_mlp1, w_mlp2, w_in_even, w_a2, b_a2, gla_norm_g, pool_w, pool_s,
                       w_out_even, w_in_odd, conv_w, conv_b, w_out_odd)
    n_dec = c.shape[0]
    cond = jnp.concatenate([c_ctx[None, :], c, jnp.zeros((COND_ROWS - 1 - n_dec, D_MODEL), F32)], axis=0)
    mods = _ada(cond, w_ada, b_ada).reshape(DEPTH * COND_ROWS, 1, 6 * D_MODEL)

    y_prompt, new_state = _trunk(x_prompt, None, mods, None, True, True, wp)
    pos = _grid_pos_embed(x_sample.shape[1]).astype(x_sample.dtype)
    y_sample, _ = _trunk(x_sample, pos, mods, state_gla, False, False, wp)
    return (y_prompt, y_sample, new_state.astype(x_prompt.dtype))
```

```python
import functools
import math

import jax
import jax.numpy as jnp
from jax import lax
from jax.experimental import pallas as pl
from jax.experimental.pallas import tpu as pltpu

D_MODEL = 1024
DEPTH = 4
N_EVEN = 2
GRID_W = 64
MIX_A = 512
MIX_B = 512
GLA_HEADS = 4
GLA_DV = 128
GLA_DK = 64
GLA_QK = 256
LOG2_DK = 6
LOG2_DV = 7
GATE_RANK = 16
GATE_TAU = 16.0
POOL_WINDOWS = (2, 4, 8, 16)
POOL_GW = 128
MIX_C = 512
MIX_D = 512
FOURIER_GW = 128
D_FF = 4096
EPS = 1e-6

LANES = 128
SUBLANES = 8
TOKEN_TILE = 512
GLA_CHUNK = 256
EVEN_MAIN = 2 * GLA_QK + 2 * MIX_A
VMEM_LIMIT = 56 * 1024 * 1024

F32 = jnp.float32
BF16 = jnp.bfloat16


def _dot(a, b):
    return jnp.dot(a, b, preferred_element_type=F32)


def _dot_nt(a, b):
    return lax.dot_general(a, b, (((1,), (1,)), ((), ())), preferred_element_type=F32)


def _split_dot(m, x):
    hi = x.astype(BF16)
    lo = (x - hi.astype(F32)).astype(BF16)
    return _dot(m, hi) + _dot(m, lo)


def _silu(x):
    return x * (1.0 / (1.0 + jnp.exp(-x)))


def _rmsnorm(x, g):
    ms = jnp.mean(x * x, axis=-1, keepdims=True)
    return (x * lax.rsqrt(ms + EPS)) * g


def _params(n_axes):
    return pltpu.CompilerParams(dimension_semantics=("arbitrary",) * n_axes, vmem_limit_bytes=VMEM_LIMIT)


def _resident(shape, layer=None):
    zeros = (0,) * len(shape)
    if layer is None:
        return pl.BlockSpec(shape, lambda *_: zeros, pipeline_mode=pl.Buffered(1))
    return pl.BlockSpec((None,) + tuple(shape), lambda *_: (layer,) + zeros, pipeline_mode=pl.Buffered(1))


ADA_COLS = 3072
COND_ROWS = SUBLANES


def _ada_kernel(cond_ref, w_ref, b_ref, out_ref):
    s = _silu(cond_ref[...]).astype(BF16)
    out_ref[0] = _dot(s, w_ref[0].astype(BF16)) + b_ref[0]


def _ada(cond, w_ada, b_ada):
    n_out = w_ada.shape[-1]
    return pl.pallas_call(
        _ada_kernel,
        grid=(DEPTH, n_out // ADA_COLS),
        in_specs=[
            pl.BlockSpec((COND_ROWS, D_MODEL), lambda l, j: (0, 0)),
            pl.BlockSpec((1, D_MODEL, ADA_COLS), lambda l, j: (l, 0, j)),
            pl.BlockSpec((1, 1, ADA_COLS), lambda l, j: (l, 0, j)),
        ],
        out_specs=pl.BlockSpec((1, COND_ROWS, ADA_COLS), lambda l, j: (l, 0, j)),
        out_shape=jax.ShapeDtypeStruct((DEPTH, COND_ROWS, n_out), F32),
        compiler_params=_params(2),
        name="ada_mod",
    )(cond, w_ada, b_ada.reshape(DEPTH, 1, n_out))


def _modulated_norm(x, mod_ref, g_ref, which):
    shift = mod_ref[0, :, (3 * which) * D_MODEL:(3 * which + 1) * D_MODEL]
    scale = mod_ref[0, :, (3 * which + 1) * D_MODEL:(3 * which + 2) * D_MODEL]
    return _rmsnorm(x, g_ref[...]) * (1.0 + scale) + shift


def _load_x(x_ref, pos_ref):
    return x_ref[...] if pos_ref is None else x_ref[...] + pos_ref[...]


HALO = SUBLANES
HALO_TILE = TOKEN_TILE + 2 * HALO
MAIN = slice(HALO, HALO + TOKEN_TILE)


def _norm_with_halo(refs, has_pos, seq_len, pos_cols):
    x_refs, refs = refs[:3], refs[3:]
    x = jnp.concatenate([r[...] for r in x_refs], axis=0)
    if has_pos:
        pos_refs, refs = refs[:3], refs[3:]
        x = x + jnp.concatenate([r[...] for r in pos_refs], axis=0)
    mod_ref, g_ref, refs = refs[0], refs[1], refs[2:]
    h = _modulated_norm(x, mod_ref, g_ref, 0).astype(BF16)
    row = lax.broadcasted_iota(jnp.int32, (HALO_TILE, pos_cols), 0)
    seq_pos = (pl.program_id(0) * TOKEN_TILE - HALO + row) & (seq_len - 1)
    return refs, h, seq_pos


def _own_sequence_only(x, seq_pos, seq_len):
    if seq_len < TOKEN_TILE:
        return x, seq_pos
    first = pl.program_id(0) * TOKEN_TILE
    lo = jnp.where((first & (seq_len - 1)) != 0, 0, HALO)
    hi = jnp.where(((first + TOKEN_TILE) & (seq_len - 1)) != 0, HALO_TILE, HALO + TOKEN_TILE)
    row = lax.broadcasted_iota(jnp.int32, x.shape, 0)
    return jnp.where((row >= lo) & (row < hi), x, 0.0), None


def _shift_down(x, s, seq_pos):
    rolled = pltpu.roll(x, s, axis=0)
    return rolled if seq_pos is None else jnp.where(seq_pos >= s, rolled, 0.0)


def _shift_up(x, s, seq_pos, seq_len):
    rolled = pltpu.roll(x, x.shape[0] - s, axis=0)
    return rolled if seq_pos is None else jnp.where(seq_pos < seq_len - s, rolled, 0.0)


def _in_even_kernel(*refs, has_pos, seq_len):
    refs, h, seq_pos = _norm_with_halo(refs, has_pos, seq_len, POOL_GW)
    (w_ref, wu_ref, wr_ref, wgate_ref, bgate_ref, pw_ref, ps_ref,
     qk_ref, v_ref, gate_ref, lf_ref, pool_ref) = refs
    h_main = h[MAIN]
    p = _dot(h_main, w_ref[...])
    qk_ref[:, :GLA_QK] = p[:, :GLA_QK] * (GLA_DK ** -0.5)
    qk_ref[:, GLA_QK:] = p[:, GLA_QK:2 * GLA_QK]
    v_ref[...] = p[:, 512:1024].astype(BF16)
    gate_ref[...] = p[:, 1024:1536]
    pre = _dot(_dot(h_main, wr_ref[...]).astype(BF16), wgate_ref[...]) + bgate_ref[...]
    log_sig = jnp.minimum(pre, 0.0) - jnp.log(1.0 + jnp.exp(-jnp.abs(pre)))
    lf_ref[...] = log_sig / GATE_TAU

    u_all, shift_pos = _own_sequence_only(_dot(h, wu_ref[...]), seq_pos, seq_len)
    for gi, w in enumerate(POOL_WINDOWS):
        sl = slice(gi * POOL_GW, (gi + 1) * POOL_GW)
        u = u_all[:, sl]
        half = w // 2
        trail, lead, m = u, u, 1
        while m < half:
            trail = trail + _shift_down(trail, m, shift_pos)
            lead = lead + _shift_up(lead, m, shift_pos, seq_len)
            m *= 2
        window = _shift_down(trail, 1, shift_pos) + lead
        cnt = (jnp.minimum(seq_pos + (w - half), seq_len) - jnp.maximum(seq_pos - half, 0)).astype(F32)
        pooled = (window / cnt - u)[MAIN]
        y = _dot(pooled.astype(BF16), pw_ref[gi])
        pool_ref[:, sl] = (y * ps_ref[:, sl]).astype(BF16)


def _in_odd_kernel(*refs, seq_len):
    refs, h, seq_pos = _norm_with_halo(refs, False, seq_len, MIX_D)
    w_ref, cw_ref, cb_ref, f_ref, conv_ref = refs
    h_main = h[MAIN]
    f_ref[...] = _dot(h_main, w_ref[:, :MIX_C]).astype(BF16)
    z = _dot(h, w_ref[:, MIX_C + 2 * MIX_D:]) * _dot(h, w_ref[:, MIX_C:MIX_C + MIX_D])
    conv = cb_ref[...] + _shift_down(z, 1, seq_pos)[MAIN] * cw_ref[0:1, :]
    conv = conv + z[MAIN] * cw_ref[1:2, :]
    conv = conv + _shift_up(z, 1, seq_pos, seq_len)[MAIN] * cw_ref[2:3, :]
    conv_ref[...] = (_dot(h_main, w_ref[:, MIX_C + MIX_D:MIX_C + 2 * MIX_D]) * conv).astype(BF16)


def _mod_spec(layer, seq_len, shared):
    base = layer * COND_ROWS
    if shared:
        return pl.BlockSpec((1, 1, 6 * D_MODEL), lambda i: (base, 0, 0))
    return pl.BlockSpec((1, 1, 6 * D_MODEL), lambda i: (base + 1 + i * TOKEN_TILE // seq_len, 0, 0))


def _tile_spec(cols):
    return pl.BlockSpec((TOKEN_TILE, cols), lambda i: (i, 0))


def _pos_spec(seq_len):
    tiles = seq_len // TOKEN_TILE
    return pl.BlockSpec((TOKEN_TILE, D_MODEL), lambda i: (i % tiles, 0))


def _halo_specs(n_rows, period_tiles):
    per_tile = TOKEN_TILE // HALO
    last = n_rows // HALO - 1
    return [
        pl.BlockSpec((HALO, D_MODEL), lambda i: (jnp.maximum((i % period_tiles) * per_tile - 1, 0), 0)),
        pl.BlockSpec((TOKEN_TILE, D_MODEL), lambda i: (i % period_tiles, 0)),
        pl.BlockSpec((HALO, D_MODEL), lambda i: (jnp.minimum((i % period_tiles + 1) * per_tile, last), 0)),
    ]


def _in_even(x, pos, mods, wp, layer, seq_len, shared):
    n = x.shape[0]
    j = layer // 2
    outs = [(512, F32), (512, BF16), (512, F32), (512, F32), (MIX_B, BF16)]
    has_pos = pos is not None
    in_specs = _halo_specs(n, n // TOKEN_TILE) + (_halo_specs(seq_len, seq_len // TOKEN_TILE) if has_pos else []) + [
        _mod_spec(layer, seq_len, shared), _resident((1, D_MODEL), 2 * layer),
        _resident((D_MODEL, EVEN_MAIN), j), _resident((D_MODEL, MIX_B), j), _resident((D_MODEL, LANES), j),
        _resident((LANES, 2 * GLA_QK), j), _resident((1, 2 * GLA_QK), j),
        _resident((len(POOL_WINDOWS), POOL_GW, POOL_GW), j), _resident((1, MIX_B), j),
    ]
    args = [x] * 3 + ([pos] * 3 if has_pos else []) + [
        mods, wp['norm_g'], wp['w_even'], wp['w_pool_in'], wp['w_rank'], wp['wgate'], wp['bgate'],
        wp['pool_w'], wp['pool_s']]
    return pl.pallas_call(
        functools.partial(_in_even_kernel, has_pos=has_pos, seq_len=seq_len),
        grid=(n // TOKEN_TILE,),
        in_specs=in_specs,
        out_specs=[_tile_spec(c) for c, _ in outs],
        out_shape=[jax.ShapeDtypeStruct((n, c), dt) for c, dt in outs],
        compiler_params=_params(1),
        name="in_even",
    )(*args)


def _in_odd(x, mods, wp, layer, seq_len, shared):
    n = x.shape[0]
    j = layer // 2
    outs = [(MIX_C, BF16), (MIX_D, BF16)]
    return pl.pallas_call(
        functools.partial(_in_odd_kernel, seq_len=seq_len),
        grid=(n // TOKEN_TILE,),
        in_specs=_halo_specs(n, n // TOKEN_TILE) + [
            _mod_spec(layer, seq_len, shared), _resident((1, D_MODEL), 2 * layer),
            _resident((D_MODEL, MIX_C + 3 * MIX_D), j),
            _resident(wp['conv_w'].shape[1:], j), _resident((1, MIX_D), j),
        ],
        out_specs=[_tile_spec(c) for c, _ in outs],
        out_shape=[jax.ShapeDtypeStruct((n, c), dt) for c, dt in outs],
        compiler_params=_params(1),
        name="in_odd",
    )(x, x, x, mods, wp['norm_g'], wp['w_odd'], wp['conv_w'], wp['conv_b'])


FF_BLOCK = 1024


def _post_kernel(*refs, has_pos, final):
    x_ref, refs = refs[0], refs[1:]
    pos_ref = None
    if has_pos:
        pos_ref, refs = refs[0], refs[1:]
    a_ref, b_ref, mod_ref, g_ref, wo_ref, w1_ref, w2_ref, fg_ref, out_ref = refs
    half = D_MODEL // 2
    y = _dot(a_ref[...], wo_ref[:half, :]) + _dot(b_ref[...], wo_ref[half:, :])
    gate1 = mod_ref[0, :, 2 * D_MODEL:3 * D_MODEL]
    gate2 = mod_ref[0, :, 5 * D_MODEL:6 * D_MODEL]
    x1 = _load_x(x_ref, pos_ref) + gate1 * y
    h = _modulated_norm(x1, mod_ref, g_ref, 1).astype(BF16)
    acc = jnp.zeros_like(x1)
    for c in range(D_FF // FF_BLOCK):
        a = jnp.maximum(_dot(h, w1_ref[:, c * FF_BLOCK:(c + 1) * FF_BLOCK]), 0.0)
        acc = acc + _dot((a * a).astype(BF16), w2_ref[c * FF_BLOCK:(c + 1) * FF_BLOCK, :])
    x2 = x1 + gate2 * acc
    if final:
        x2 = _rmsnorm(x2, fg_ref[...])
    out_ref[...] = x2


def _post(x, pos, a, b, mods, wp, layer, seq_len, shared):
    n = x.shape[0]
    j = layer // 2
    final = layer == DEPTH - 1
    has_pos = pos is not None
    w_out = wp['w_out_even'] if layer % 2 == 0 else wp['w_out_odd']
    in_specs = [_tile_spec(D_MODEL)] + ([_pos_spec(seq_len)] if has_pos else []) + [
        _tile_spec(D_MODEL // 2), _tile_spec(D_MODEL // 2),
        _mod_spec(layer, seq_len, shared), _resident((1, D_MODEL), 2 * layer + 1),
        _resident((D_MODEL, D_MODEL), j), _resident((D_MODEL, D_FF), layer), _resident((D_FF, D_MODEL), layer),
        _resident((1, D_MODEL)),
    ]
    args = [x] + ([pos] if has_pos else []) + [a, b, mods, wp['norm_g'], w_out, wp['w1'], wp['w2'], wp['final_g']]
    return pl.pallas_call(
        functools.partial(_post_kernel, has_pos=has_pos, final=final),
        grid=(n // TOKEN_TILE,),
        in_specs=in_specs,
        out_specs=_tile_spec(D_MODEL),
        out_shape=jax.ShapeDtypeStruct((n, D_MODEL), F32),
        compiler_params=_params(1),
        name="post_final" if final else "post",
    )(*args)


def _head_block_mask():
    r = lax.broadcasted_iota(jnp.int32, (GLA_QK, MIX_A), 0) >> LOG2_DK
    c = lax.broadcasted_iota(jnp.int32, (GLA_QK, MIX_A), 1) >> LOG2_DV
    return jnp.where(r == c, 1.0, 0.0).astype(F32)


def _gla_chunk(qk_ref, v_ref, lf_ref, state, backward):
    C = GLA_CHUNK
    q = qk_ref[:, :GLA_QK]
    k = qk_ref[:, GLA_QK:]
    v = v_ref[...]
    lf = lf_ref[...]

    row = lax.broadcasted_iota(jnp.int32, (C, C), 0)
    col = lax.broadcasted_iota(jnp.int32, (C, C), 1)
    causal = (col >= row) if backward else (col <= row)
    cum = _split_dot(jnp.where(causal, 1.0, 0.0).astype(BF16), lf)
    total_row = cum[0:1, :] if backward else cum[C - 1:C, :]
    mid = cum[C // 2:C // 2 + 1, :]

    q_mid = q * jnp.exp(cum - mid)
    k_mid = k * jnp.exp(mid - cum)
    o = _dot((q_mid * jnp.exp(mid)).astype(BF16), state.astype(BF16))

    q_c = q_mid.astype(BF16)
    k_c = k_mid.astype(BF16)
    lane_head = lax.broadcasted_iota(jnp.int32, (1, GLA_QK), 1) >> LOG2_DK
    intra = []
    for h in range(GLA_HEADS):
        k_h = k_c * jnp.where(lane_head == h, 1.0, 0.0).astype(BF16)
        att = jnp.where(causal, _dot_nt(q_c, k_h), 0.0).astype(BF16)
        intra.append(_dot(att, v[:, h * GLA_DV:(h + 1) * GLA_DV]))
    o = o + jnp.concatenate(intra, axis=1)

    k_out = (k_mid * jnp.exp(total_row - mid)).T.astype(BF16)
    decay = jnp.exp(jnp.sum(lf.T, axis=1, keepdims=True))
    mask = _head_block_mask()
    new_state = state * decay + _dot(k_out, v) * mask
    return o, new_state


def _gla_kernel(*refs, n_chunks, has_state, want_state):
    (qkf_ref, vf_ref, lff_ref, qkb_ref, vb_ref, lfb_ref, gate_ref, gn_ref), refs = refs[:8], refs[8:]
    s0_ref = None
    if has_state:
        s0_ref, refs = refs[0], refs[1:]
    o_ref, refs = refs[0], refs[1:]
    sout_ref = None
    if want_state:
        sout_ref, refs = refs[0], refs[1:]
    state_ref, part_f_ref, part_b_ref = refs

    C = GLA_CHUNK
    step = pl.program_id(1)

    @pl.when(step == 0)
    def _():
        for d in range(2):
            if has_state:
                state_ref[d] = jnp.concatenate([s0_ref[0, d]] * GLA_HEADS, axis=1) * _head_block_mask()
            else:
                state_ref[d] = jnp.zeros((GLA_QK, MIX_A), F32)

    o_f, state_f = _gla_chunk(qkf_ref, vf_ref, lff_ref, state_ref[0], backward=False)
    o_b, state_b = _gla_chunk(qkb_ref, vb_ref, lfb_ref, state_ref[1], backward=True)
    state_ref[0] = state_f
    state_ref[1] = state_b

    rows_f = pl.ds(pl.multiple_of(step * C, C), C)
    rows_b = pl.ds(pl.multiple_of((n_chunks - 1 - step) * C, C), C)
    part_f_ref[rows_f, :] = o_f
    part_b_ref[rows_b, :] = o_b

    def finish(rows):
        both = part_f_ref[rows, :] + part_b_ref[rows, :]
        normed = []
        for h in range(GLA_HEADS):
            o_h = both[:, h * GLA_DV:(h + 1) * GLA_DV]
            ms = jnp.mean(o_h * o_h, axis=-1, keepdims=True)
            normed.append(o_h * lax.rsqrt(ms + EPS) * gn_ref[...])
        o_ref[rows, :] = (jnp.concatenate(normed, axis=1) * _silu(gate_ref[rows, :])).astype(BF16)

    @pl.when(2 * step >= n_chunks - 1)
    def _():
        finish(rows_f)

    @pl.when(2 * step > n_chunks - 1)
    def _():
        finish(rows_b)

    if want_state:
        @pl.when(step == n_chunks - 1)
        def _():
            for d, st in enumerate((state_f, state_b)):
                s = st[:, :GLA_DV]
                for h in range(1, GLA_HEADS):
                    s = s + st[:, h * GLA_DV:(h + 1) * GLA_DV]
                sout_ref[0, d] = s


def _gla(qk, v, lf, gate, wp, layer, s0, batch, seq_len, want_state):
    n = qk.shape[0]
    n_chunks = seq_len // GLA_CHUNK
    has_state = s0 is not None

    def fwd_map(col):
        return lambda b, s: (b * n_chunks + s, col)

    def bwd_map(col):
        return lambda b, s: (b * n_chunks + n_chunks - 1 - s, col)

    seq_spec = pl.BlockSpec((seq_len, MIX_A), lambda b, s: (b, 0))
    state_spec = pl.BlockSpec((1, 2, GLA_QK, GLA_DV), lambda b, s: (b, 0, 0, 0))
    in_specs = []
    for chunk_map in (fwd_map, bwd_map):
        in_specs += [
            pl.BlockSpec((GLA_CHUNK, 2 * GLA_QK), chunk_map(0)),
            pl.BlockSpec((GLA_CHUNK, MIX_A), chunk_map(0)),
            pl.BlockSpec((GLA_CHUNK, GLA_QK), chunk_map(0 if chunk_map is fwd_map else 1)),
        ]
    in_specs += [seq_spec, _resident((1, GLA_DV), layer // 2)]
    args = [qk, v, lf, qk, v, lf, gate, wp['gn']]
    if has_state:
        in_specs.append(state_spec)
        args.append(s0)
    out_specs = [seq_spec]
    out_shape = [jax.ShapeDtypeStruct((n, MIX_A), BF16)]
    if want_state:
        out_specs.append(state_spec)
        out_shape.append(jax.ShapeDtypeStruct((batch, 2, GLA_QK, GLA_DV), F32))
    outs = pl.pallas_call(
        functools.partial(_gla_kernel, n_chunks=n_chunks, has_state=has_state, want_state=want_state),
        grid=(batch, n_chunks),
        in_specs=in_specs,
        out_specs=out_specs,
        out_shape=out_shape,
        scratch_shapes=[pltpu.VMEM((2, GLA_QK, MIX_A), F32), pltpu.VMEM((seq_len, MIX_A), F32),
                        pltpu.VMEM((seq_len, MIX_A), F32)],
        compiler_params=_params(2),
        name="gla",
    )(*args)
    return (outs[0], outs[1]) if want_state else (outs[0], None)


DFT_ROWS = 512
DFT_LOW = 32


def _dft_rows(rows, n):
    k = jnp.arange(n, dtype=jnp.int32)
    ang = ((rows[:, None] * k[None, :]) % n).astype(F32) * (2.0 * math.pi / n)
    return jnp.cos(ang), jnp.sin(ang)


def _dft_tables(n):
    c_lo, s_lo = _dft_rows(jnp.arange(DFT_LOW, dtype=jnp.int32), n)
    c_hi, s_hi = _dft_rows(jnp.arange(n // DFT_LOW, dtype=jnp.int32) * DFT_LOW, n)
    cos = c_hi[:, None, :] * c_lo[None, :, :] - s_hi[:, None, :] * s_lo[None, :, :]
    sin = s_hi[:, None, :] * c_lo[None, :, :] + c_hi[:, None, :] * s_lo[None, :, :]
    return cos.reshape(n, n), sin.reshape(n, n)


def _fourier_kernel(f_ref, cs_ref, cl_ref, sl_ref, out_ref, xc_ref, xs_ref, *, scale):
    n = f_ref.shape[0]
    for g in range(MIX_C // FOURIER_GW):
        sl = slice(g * FOURIER_GW, (g + 1) * FOURIER_GW)
        x = _dot(f_ref[:, sl], cs_ref[...])
        xc_ref[:, sl] = x[:, :FOURIER_GW].astype(BF16)
        xs_ref[:, sl] = x[:, FOURIER_GW:].astype(BF16)

    block = min(DFT_ROWS, n)

    def body(r, carry):
        rows = pl.ds(pl.multiple_of(r * block, block), block)
        re = _dot(cl_ref[rows, :], xc_ref[...]) - _dot(sl_ref[rows, :], xs_ref[...])
        out_ref[rows, :] = (re * scale).astype(BF16)
        return carry

    lax.fori_loop(0, n // block, body, 0)


def _fourier_tables(seq_len):
    cc, sc = _dft_rows(jnp.arange(FOURIER_GW, dtype=jnp.int32), FOURIER_GW)
    cl, sl = _dft_tables(seq_len)
    return jnp.concatenate([cc, sc], axis=1).astype(BF16), cl.astype(BF16), sl.astype(BF16)


def _fourier(f, tables, batch, seq_len):
    n = f.shape[0]
    scale = 1.0 / math.sqrt(seq_len * FOURIER_GW)
    seq_spec = pl.BlockSpec((seq_len, MIX_C), lambda b: (b, 0))
    return pl.pallas_call(
        functools.partial(_fourier_kernel, scale=scale),
        grid=(batch,),
        in_specs=[seq_spec, _resident((FOURIER_GW, 2 * FOURIER_GW)),
                  _resident((seq_len, seq_len)), _resident((seq_len, seq_len))],
        out_specs=seq_spec,
        out_shape=jax.ShapeDtypeStruct((n, MIX_C), BF16),
        scratch_shapes=[pltpu.VMEM((seq_len, MIX_C), BF16), pltpu.VMEM((seq_len, MIX_C), BF16)],
        compiler_params=_params(1),
        name="fourier",
    )(f, *tables)


def _grid_pos_embed(n_tok):
    rows = n_tok // GRID_W
    quarter = D_MODEL // 4
    freqs = 1.0 / (10000.0 ** (jnp.arange(quarter, dtype=F32) / quarter))
    ar = jnp.arange(rows, dtype=F32)[:, None] * freqs
    ac = jnp.arange(GRID_W, dtype=F32)[:, None] * freqs

    def per_row(t):
        return jnp.broadcast_to(t[:, None, :], (rows, GRID_W, quarter))

    def per_col(t):
        return jnp.broadcast_to(t[None, :, :], (rows, GRID_W, quarter))

    pos = jnp.concatenate([per_row(jnp.sin(ar)), per_row(jnp.cos(ar)), per_col(jnp.sin(ac)), per_col(jnp.cos(ac))],
                          axis=-1)
    return pos.reshape(n_tok, D_MODEL)


def _prep_weights(norm_g, final_g, w_mlp1, w_mlp2, w_in_even, w_a2, b_a2, gla_norm_g, pool_w, pool_s,
                  w_out_even, w_in_odd, conv_w, conv_b, w_out_odd):
    rank_end = EVEN_MAIN + 2 * GATE_RANK
    w_even = w_in_even[:, :, :EVEN_MAIN].astype(BF16)
    w_pool_in = w_in_even[:, :, rank_end:].astype(BF16)
    w_rank = jnp.pad(w_in_even[:, :, EVEN_MAIN:rank_end].astype(BF16),
                     ((0, 0), (0, 0), (0, LANES - 2 * GATE_RANK)))
    wgate = jnp.zeros((N_EVEN, LANES, 2 * GLA_QK), F32)
    wgate = wgate.at[:, :GATE_RANK, :GLA_QK].set(w_a2[:, 0])
    wgate = wgate.at[:, GATE_RANK:2 * GATE_RANK, GLA_QK:].set(w_a2[:, 1])
    return dict(
        norm_g=norm_g.reshape(DEPTH * 2, 1, D_MODEL),
        final_g=final_g.reshape(1, D_MODEL),
        w1=w_mlp1.astype(BF16), w2=w_mlp2.astype(BF16),
        w_even=w_even, w_pool_in=w_pool_in, w_rank=w_rank, wgate=wgate.astype(BF16), bgate=b_a2.reshape(N_EVEN, 1, 2 * GLA_QK),
        gn=gla_norm_g.reshape(N_EVEN, 1, GLA_DV),
        pool_w=pool_w.astype(BF16), pool_s=pool_s.reshape(N_EVEN, 1, MIX_B),
        w_out_even=w_out_even.astype(BF16),
        w_odd=w_in_odd.astype(BF16), conv_w=conv_w, conv_b=conv_b.reshape(-1, 1, MIX_D),
        w_out_odd=w_out_odd.astype(BF16),
    )


def _trunk(x, pos, mods, gla_init, shared_mod, want_state, wp):
    batch, seq_len, _ = x.shape
    x = x.reshape(batch * seq_len, D_MODEL)
    tables = _fourier_tables(seq_len)
    states = []
    for l in range(DEPTH):
        j = l // 2
        if l % 2 == 0:
            qk, v, gate, lf, b = _in_even(x, pos, mods, wp, l, seq_len, shared_mod)
            s0 = None if gla_init is None else gla_init[:, j].reshape(batch, 2, GLA_QK, GLA_DV)
            a, s = _gla(qk, v, lf, gate, wp, l, s0, batch, seq_len, want_state)
            if want_state:
                states.append(s.reshape(batch, 2, GLA_HEADS, GLA_DK, GLA_DV))
        else:
            f, b = _in_odd(x, mods, wp, l, seq_len, shared_mod)
            a = _fourier(f, tables, batch, seq_len)
        x = _post(x, pos, a, b, mods, wp, l, seq_len, shared_mod)
        pos = None
    y = x.reshape(batch, seq_len, D_MODEL)
    return y, (jnp.stack(states, axis=1) if want_state else None)


def kernel(x_prompt, x_sample, state_gla, c, c_ctx, norm_g, final_g, w_ada, b_ada, w_mlp1, w_mlp2,
           w_in_even, w_a2, b_a2, gla_norm_g, pool_w, pool_s, w_out_even,
           w_in_odd, conv_w, conv_b, w_out_odd):
    wp = _prep_weights(norm_g, final_g, w_mlp1, w_mlp2, w_in_even, w_a2, b_a2, gla_norm_g, pool_w, pool_s,
                       w_out_even, w_in_odd, conv_w, conv_b, w_out_odd)
    n_dec = c.shape[0]
    cond = jnp.concatenate([c_ctx[None, :], c, jnp.zeros((COND_ROWS - 1 - n_dec, D_MODEL), F32)], axis=0)
    mods = _ada(cond, w_ada, b_ada).reshape(DEPTH * COND_ROWS, 1, 6 * D_MODEL)

    y_prompt, new_state = _trunk(x_prompt, None, mods, None, True, True, wp)
    pos = _grid_pos_embed(x_sample.shape[1]).astype(x_sample.dtype)
    y_sample, _ = _trunk(x_sample, pos, mods, state_gla, False, False, wp)
    return (y_prompt, y_sample, new_state.astype(x_prompt.dtype))
```

```python
import functools
import math

import jax
import jax.numpy as jnp
from jax import lax
from jax.experimental import pallas as pl
from jax.experimental.pallas import tpu as pltpu

D_MODEL = 1024
DEPTH = 4
N_EVEN = 2
GRID_W = 64
MIX_A = 512
MIX_B = 512
GLA_HEADS = 4
GLA_DV = 128
GLA_DK = 64
GLA_QK = 256
LOG2_DK = 6
LOG2_DV = 7
GATE_RANK = 16
GATE_TAU = 16.0
POOL_WINDOWS = (2, 4, 8, 16)
POOL_GW = 128
MIX_C = 512
MIX_D = 512
FOURIER_GW = 128
D_FF = 4096
EPS = 1e-6

LANES = 128
SUBLANES = 8
TOKEN_TILE = 512
GLA_CHUNK = 256
EVEN_MAIN = 2 * GLA_QK + 2 * MIX_A
VMEM_LIMIT = 56 * 1024 * 1024

F32 = jnp.float32
BF16 = jnp.bfloat16


def _dot(a, b):
    return jnp.dot(a, b, preferred_element_type=F32)


def _dot_nt(a, b):
    return lax.dot_general(a, b, (((1,), (1,)), ((), ())), preferred_element_type=F32)


def _split_dot(m, x):
    hi = x.astype(BF16)
    lo = (x - hi.astype(F32)).astype(BF16)
    return _dot(m, hi) + _dot(m, lo)


def _silu(x):
    return x * (1.0 / (1.0 + jnp.exp(-x)))


def _rmsnorm(x, g):
    ms = jnp.mean(x * x, axis=-1, keepdims=True)
    return (x * lax.rsqrt(ms + EPS)) * g


def _params(n_axes):
    return pltpu.CompilerParams(dimension_semantics=("arbitrary",) * n_axes, vmem_limit_bytes=VMEM_LIMIT)


def _resident(shape, layer=None):
    zeros = (0,) * len(shape)
    if layer is None:
        return pl.BlockSpec(shape, lambda *_: zeros, pipeline_mode=pl.Buffered(1))
    return pl.BlockSpec((None,) + tuple(shape), lambda *_: (layer,) + zeros, pipeline_mode=pl.Buffered(1))


ADA_COLS = 3072
COND_ROWS = SUBLANES


def _ada_kernel(cond_ref, w_ref, b_ref, out_ref):
    s = _silu(cond_ref[...]).astype(BF16)
    out_ref[0] = _dot(s, w_ref[0].astype(BF16)) + b_ref[0]


def _ada(cond, w_ada, b_ada):
    n_out = w_ada.shape[-1]
    return pl.pallas_call(
        _ada_kernel,
        grid=(DEPTH, n_out // ADA_COLS),
        in_specs=[
            pl.BlockSpec((COND_ROWS, D_MODEL), lambda l, j: (0, 0)),
            pl.BlockSpec((1, D_MODEL, ADA_COLS), lambda l, j: (l, 0, j)),
            pl.BlockSpec((1, 1, ADA_COLS), lambda l, j: (l, 0, j)),
        ],
        out_specs=pl.BlockSpec((1, COND_ROWS, ADA_COLS), lambda l, j: (l, 0, j)),
        out_shape=jax.ShapeDtypeStruct((DEPTH, COND_ROWS, n_out), F32),
        compiler_params=_params(2),
        name="ada_mod",
    )(cond, w_ada, b_ada.reshape(DEPTH, 1, n_out))


def _modulated_norm(x, mod_ref, g_ref, which):
    shift = mod_ref[0, :, (3 * which) * D_MODEL:(3 * which + 1) * D_MODEL]
    scale = mod_ref[0, :, (3 * which + 1) * D_MODEL:(3 * which + 2) * D_MODEL]
    return _rmsnorm(x, g_ref[...]) * (1.0 + scale) + shift


def _load_x(x_ref, pos_ref):
    return x_ref[...] if pos_ref is None else x_ref[...] + pos_ref[...]


HALO = SUBLANES
HALO_TILE = TOKEN_TILE + 2 * HALO
MAIN = slice(HALO, HALO + TOKEN_TILE)


def _norm_with_halo(refs, has_pos, seq_len, pos_cols):
    x_refs, refs = refs[:3], refs[3:]
    x = jnp.concatenate([r[...] for r in x_refs], axis=0)
    if has_pos:
        pos_refs, refs = refs[:3], refs[3:]
        x = x + jnp.concatenate([r[...] for r in pos_refs], axis=0)
    mod_ref, g_ref, refs = refs[0], refs[1], refs[2:]
    h = _modulated_norm(x, mod_ref, g_ref, 0).astype(BF16)
    row = lax.broadcasted_iota(jnp.int32, (HALO_TILE, pos_cols), 0)
    seq_pos = (pl.program_id(0) * TOKEN_TILE - HALO + row) & (seq_len - 1)
    return refs, h, seq_pos


def _own_sequence_only(x, seq_pos, seq_len):
    if seq_len < TOKEN_TILE:
        return x, seq_pos
    first = pl.program_id(0) * TOKEN_TILE
    lo = jnp.where((first & (seq_len - 1)) != 0, 0, HALO)
    hi = jnp.where(((first + TOKEN_TILE) & (seq_len - 1)) != 0, HALO_TILE, HALO + TOKEN_TILE)
    row = lax.broadcasted_iota(jnp.int32, x.shape, 0)
    return jnp.where((row >= lo) & (row < hi), x, 0.0), None


def _shift_down(x, s, seq_pos):
    rolled = pltpu.roll(x, s, axis=0)
    return rolled if seq_pos is None else jnp.where(seq_pos >= s, rolled, 0.0)


def _shift_up(x, s, seq_pos, seq_len):
    rolled = pltpu.roll(x, x.shape[0] - s, axis=0)
    return rolled if seq_pos is None else jnp.where(seq_pos < seq_len - s, rolled, 0.0)


def _in_even_kernel(*refs, has_pos, seq_len):
    refs, h, seq_pos = _norm_with_halo(refs, has_pos, seq_len, POOL_GW)
    (w_ref, wu_ref, wr_ref, wgate_ref, bgate_ref, pw_ref, ps_ref,
     qk_ref, v_ref, gate_ref, lf_ref, pool_ref) = refs
    h_main = h[MAIN]
    p = _dot(h_main, w_ref[...])
    qk_ref[:, :GLA_QK] = p[:, :GLA_QK] * (GLA_DK ** -0.5)
    qk_ref[:, GLA_QK:] = p[:, GLA_QK:2 * GLA_QK]
    v_ref[...] = p[:, 512:1024].astype(BF16)
    gate_ref[...] = p[:, 1024:1536]
    pre = _dot(_dot(h_main, wr_ref[...]).astype(BF16), wgate_ref[...]) + bgate_ref[...]
    log_sig = jnp.minimum(pre, 0.0) - jnp.log(1.0 + jnp.exp(-jnp.abs(pre)))
    lf_ref[...] = log_sig / GATE_TAU

    u_all, shift_pos = _own_sequence_only(_dot(h, wu_ref[...]), seq_pos, seq_len)
    for gi, w in enumerate(POOL_WINDOWS):
        sl = slice(gi * POOL_GW, (gi + 1) * POOL_GW)
        u = u_all[:, sl]
        half = w // 2
        trail, lead, m = u, u, 1
        while m < half:
            trail = trail + _shift_down(trail, m, shift_pos)
            lead = lead + _shift_up(lead, m, shift_pos, seq_len)
            m *= 2
        window = _shift_down(trail, 1, shift_pos) + lead
        cnt = (jnp.minimum(seq_pos + (w - half), seq_len) - jnp.maximum(seq_pos - half, 0)).astype(F32)
        pooled = (window / cnt - u)[MAIN]
        y = _dot(pooled.astype(BF16), pw_ref[gi])
        pool_ref[:, sl] = (y * ps_ref[:, sl]).astype(BF16)


def _in_odd_kernel(*refs, seq_len):
    refs, h, seq_pos = _norm_with_halo(refs, False, seq_len, MIX_D)
    w_ref, cw_ref, cb_ref, f_ref, conv_ref = refs
    h_main = h[MAIN]
    f_ref[...] = _dot(h_main, w_ref[:, :MIX_C]).astype(BF16)
    z = _dot(h, w_ref[:, MIX_C + 2 * MIX_D:]) * _dot(h, w_ref[:, MIX_C:MIX_C + MIX_D])
    conv = cb_ref[...] + _shift_down(z, 1, seq_pos)[MAIN] * cw_ref[0:1, :]
    conv = conv + z[MAIN] * cw_ref[1:2, :]
    conv = conv + _shift_up(z, 1, seq_pos, seq_len)[MAIN] * cw_ref[2:3, :]
    conv_ref[...] = (_dot(h_main, w_ref[:, MIX_C + MIX_D:MIX_C + 2 * MIX_D]) * conv).astype(BF16)


def _mod_spec(layer, seq_len, shared):
    base = layer * COND_ROWS
    if shared:
        return pl.BlockSpec((1, 1, 6 * D_MODEL), lambda i: (base, 0, 0))
    return pl.BlockSpec((1, 1, 6 * D_MODEL), lambda i: (base + 1 + i * TOKEN_TILE // seq_len, 0, 0))


def _tile_spec(cols):
    return pl.BlockSpec((TOKEN_TILE, cols), lambda i: (i, 0))


def _pos_spec(seq_len):
    tiles = seq_len // TOKEN_TILE
    return pl.BlockSpec((TOKEN_TILE, D_MODEL), lambda i: (i % tiles, 0))


def _halo_specs(n_rows, period_tiles):
    per_tile = TOKEN_TILE // HALO
    last = n_rows // HALO - 1
    return [
        pl.BlockSpec((HALO, D_MODEL), lambda i: (jnp.maximum((i % period_tiles) * per_tile - 1, 0), 0)),
        pl.BlockSpec((TOKEN_TILE, D_MODEL), lambda i: (i % period_tiles, 0)),
        pl.BlockSpec((HALO, D_MODEL), lambda i: (jnp.minimum((i % period_tiles + 1) * per_tile, last), 0)),
    ]


def _in_even(x, pos, mods, wp, layer, seq_len, shared):
    n = x.shape[0]
    j = layer // 2
    outs = [(512, F32), (512, BF16), (512, F32), (512, F32), (MIX_B, BF16)]
    has_pos = pos is not None
    in_specs = _halo_specs(n, n // TOKEN_TILE) + (_halo_specs(seq_len, seq_len // TOKEN_TILE) if has_pos else []) + [
        _mod_spec(layer, seq_len, shared), _resident((1, D_MODEL), 2 * layer),
        _resident((D_MODEL, EVEN_MAIN), j), _resident((D_MODEL, MIX_B), j), _resident((D_MODEL, LANES), j),
        _resident((LANES, 2 * GLA_QK), j), _resident((1, 2 * GLA_QK), j),
        _resident((len(POOL_WINDOWS), POOL_GW, POOL_GW), j), _resident((1, MIX_B), j),
    ]
    args = [x] * 3 + ([pos] * 3 if has_pos else []) + [
        mods, wp['norm_g'], wp['w_even'], wp['w_pool_in'], wp['w_rank'], wp['wgate'], wp['bgate'],
        wp['pool_w'], wp['pool_s']]
    return pl.pallas_call(
        functools.partial(_in_even_kernel, has_pos=has_pos, seq_len=seq_len),
        grid=(n // TOKEN_TILE,),
        in_specs=in_specs,
        out_specs=[_tile_spec(c) for c, _ in outs],
        out_shape=[jax.ShapeDtypeStruct((n, c), dt) for c, dt in outs],
        compiler_params=_params(1),
        name="in_even",
    )(*args)


def _in_odd(x, mods, wp, layer, seq_len, shared):
    n = x.shape[0]
    j = layer // 2
    outs = [(MIX_C, BF16), (MIX_D, BF16)]
    return pl.pallas_call(
        functools.partial(_in_odd_kernel, seq_len=seq_len),
        grid=(n // TOKEN_TILE,),
        in_specs=_halo_specs(n, n // TOKEN_TILE) + [
            _mod_spec(layer, seq_len, shared), _resident((1, D_MODEL), 2 * layer),
            _resident((D_MODEL, MIX_C + 3 * MIX_D), j),
            _resident(wp['conv_w'].shape[1:], j), _resident((1, MIX_D), j),
        ],
        out_specs=[_tile_spec(c) for c, _ in outs],
        out_shape=[jax.ShapeDtypeStruct((n, c), dt) for c, dt in outs],
        compiler_params=_params(1),
        name="in_odd",
    )(x, x, x, mods, wp['norm_g'], wp['w_odd'], wp['conv_w'], wp['conv_b'])


FF_BLOCK = 1024


def _post_kernel(*refs, has_pos, final):
    x_ref, refs = refs[0], refs[1:]
    pos_ref = None
    if has_pos:
        pos_ref, refs = refs[0], refs[1:]
    a_ref, b_ref, mod_ref, g_ref, wo_ref, w1_ref, w2_ref, fg_ref, out_ref = refs
    half = D_MODEL // 2
    y = _dot(a_ref[...], wo_ref[:half, :]) + _dot(b_ref[...], wo_ref[half:, :])
    gate1 = mod_ref[0, :, 2 * D_MODEL:3 * D_MODEL]
    gate2 = mod_ref[0, :, 5 * D_MODEL:6 * D_MODEL]
    x1 = _load_x(x_ref, pos_ref) + gate1 * y
    h = _modulated_norm(x1, mod_ref, g_ref, 1).astype(BF16)
    acc = jnp.zeros_like(x1)
    for c in range(D_FF // FF_BLOCK):
        a = jnp.maximum(_dot(h, w1_ref[:, c * FF_BLOCK:(c + 1) * FF_BLOCK]), 0.0)
        acc = acc + _dot((a * a).astype(BF16), w2_ref[c * FF_BLOCK:(c + 1) * FF_BLOCK, :])
    x2 = x1 + gate2 * acc
    if final:
        x2 = _rmsnorm(x2, fg_ref[...])
    out_ref[...] = x2


def _post(x, pos, a, b, mods, wp, layer, seq_len, shared):
    n = x.shape[0]
    j = layer // 2
    final = layer == DEPTH - 1
    has_pos = pos is not None
    w_out = wp['w_out_even'] if layer % 2 == 0 else wp['w_out_odd']
    in_specs = [_tile_spec(D_MODEL)] + ([_pos_spec(seq_len)] if has_pos else []) + [
        _tile_spec(D_MODEL // 2), _tile_spec(D_MODEL // 2),
        _mod_spec(layer, seq_len, shared), _resident((1, D_MODEL), 2 * layer + 1),
        _resident((D_MODEL, D_MODEL), j), _resident((D_MODEL, D_FF), layer), _resident((D_FF, D_MODEL), layer),
        _resident((1, D_MODEL)),
    ]
    args = [x] + ([pos] if has_pos else []) + [a, b, mods, wp['norm_g'], w_out, wp['w1'], wp['w2'], wp['final_g']]
    return pl.pallas_call(
        functools.partial(_post_kernel, has_pos=has_pos, final=final),
        grid=(n // TOKEN_TILE,),
        in_specs=in_specs,
        out_specs=_tile_spec(D_MODEL),
        out_shape=jax.ShapeDtypeStruct((n, D_MODEL), F32),
        compiler_params=_params(1),
        name="post_final" if final else "post",
    )(*args)


def _head_block_mask():
    r = lax.broadcasted_iota(jnp.int32, (GLA_QK, MIX_A), 0) >> LOG2_DK
    c = lax.broadcasted_iota(jnp.int32, (GLA_QK, MIX_A), 1) >> LOG2_DV
    return jnp.where(r == c, 1.0, 0.0).astype(F32)


def _gla_chunk(qk_ref, v_ref, lf_ref, state, backward):
    C = GLA_CHUNK
    q = qk_ref[:, :GLA_QK]
    k = qk_ref[:, GLA_QK:]
    v = v_ref[...]
    lf = lf_ref[...]

    row = lax.broadcasted_iota(jnp.int32, (C, C), 0)
    col = lax.broadcasted_iota(jnp.int32, (C, C), 1)
    causal = (col >= row) if backward else (col <= row)
    cum = _split_dot(jnp.where(causal, 1.0, 0.0).astype(BF16), lf)
    total_row = cum[0:1, :] if backward else cum[C - 1:C, :]
    mid = cum[C // 2:C // 2 + 1, :]

    q_mid = q * jnp.exp(cum - mid)
    k_mid = k * jnp.exp(mid - cum)
    o = _dot((q_mid * jnp.exp(mid)).astype(BF16), state.astype(BF16))

    q_c = q_mid.astype(BF16)
    k_c = k_mid.astype(BF16)
    lane_head = lax.broadcasted_iota(jnp.int32, (1, GLA_QK), 1) >> LOG2_DK
    intra = []
    for h in range(GLA_HEADS):
        k_h = k_c * jnp.where(lane_head == h, 1.0, 0.0).astype(BF16)
        att = jnp.where(causal, _dot_nt(q_c, k_h), 0.0).astype(BF16)
        intra.append(_dot(att, v[:, h * GLA_DV:(h + 1) * GLA_DV]))
    o = o + jnp.concatenate(intra, axis=1)

    k_out = (k_mid * jnp.exp(total_row - mid)).T.astype(BF16)
    decay = jnp.exp(jnp.sum(lf.T, axis=1, keepdims=True))
    mask = _head_block_mask()
    new_state = state * decay + _dot(k_out, v) * mask
    return o, new_state


def _gla_kernel(*refs, n_chunks, has_state, want_state):
    (qkf_ref, vf_ref, lff_ref, qkb_ref, vb_ref, lfb_ref, gate_ref, gn_ref), refs = refs[:8], refs[8:]
    s0_ref = None
    if has_state:
        s0_ref, refs = refs[0], refs[1:]
    o_ref, refs = refs[0], refs[1:]
    sout_ref = None
    if want_state:
        sout_ref, refs = refs[0], refs[1:]
    state_ref, part_f_ref, part_b_ref = refs

    C = GLA_CHUNK
    step = pl.program_id(1)

    @pl.when(step == 0)
    def _():
        for d in range(2):
            if has_state:
                state_ref[d] = jnp.concatenate([s0_ref[0, d]] * GLA_HEADS, axis=1) * _head_block_mask()
            else:
                state_ref[d] = jnp.zeros((GLA_QK, MIX_A), F32)

    o_f, state_f = _gla_chunk(qkf_ref, vf_ref, lff_ref, state_ref[0], backward=False)
    o_b, state_b = _gla_chunk(qkb_ref, vb_ref, lfb_ref, state_ref[1], backward=True)
    state_ref[0] = state_f
    state_ref[1] = state_b

    rows_f = pl.ds(pl.multiple_of(step * C, C), C)
    rows_b = pl.ds(pl.multiple_of((n_chunks - 1 - step) * C, C), C)
    part_f_ref[rows_f, :] = o_f
    part_b_ref[rows_b, :] = o_b

    def finish(rows):
        both = part_f_ref[rows, :] + part_b_ref[rows, :]
        normed = []
        for h in range(GLA_HEADS):
            o_h = both[:, h * GLA_DV:(h + 1) * GLA_DV]
            ms = jnp.mean(o_h * o_h, axis=-1, keepdims=True)
            normed.append(o_h * lax.rsqrt(ms + EPS) * gn_ref[...])
        o_ref[rows, :] = (jnp.concatenate(normed, axis=1) * _silu(gate_ref[rows, :])).astype(BF16)

    @pl.when(2 * step >= n_chunks - 1)
    def _():
        finish(rows_f)

    @pl.when(2 * step > n_chunks - 1)
    def _():
        finish(rows_b)

    if want_state:
        @pl.when(step == n_chunks - 1)
        def _():
            for d, st in enumerate((state_f, state_b)):
                s = st[:, :GLA_DV]
                for h in range(1, GLA_HEADS):
                    s = s + st[:, h * GLA_DV:(h + 1) * GLA_DV]
                sout_ref[0, d] = s


def _gla(qk, v, lf, gate, wp, layer, s0, batch, seq_len, want_state):
    n = qk.shape[0]
    n_chunks = seq_len // GLA_CHUNK
    has_state = s0 is not None

    def fwd_map(col):
        return lambda b, s: (b * n_chunks + s, col)

    def bwd_map(col):
        return lambda b, s: (b * n_chunks + n_chunks - 1 - s, col)

    seq_spec = pl.BlockSpec((seq_len, MIX_A), lambda b, s: (b, 0))
    state_spec = pl.BlockSpec((1, 2, GLA_QK, GLA_DV), lambda b, s: (b, 0, 0, 0))
    in_specs = []
    for chunk_map in (fwd_map, bwd_map):
        in_specs += [
            pl.BlockSpec((GLA_CHUNK, 2 * GLA_QK), chunk_map(0)),
            pl.BlockSpec((GLA_CHUNK, MIX_A), chunk_map(0)),
            pl.BlockSpec((GLA_CHUNK, GLA_QK), chunk_map(0 if chunk_map is fwd_map else 1)),
        ]
    in_specs += [seq_spec, _resident((1, GLA_DV), layer // 2)]
    args = [qk, v, lf, qk, v, lf, gate, wp['gn']]
    if has_state:
        in_specs.append(state_spec)
        args.append(s0)
    out_specs = [seq_spec]
    out_shape = [jax.ShapeDtypeStruct((n, MIX_A), BF16)]
    if want_state:
        out_specs.append(state_spec)
        out_shape.append(jax.ShapeDtypeStruct((batch, 2, GLA_QK, GLA_DV), F32))
    outs = pl.pallas_call(
        functools.partial(_gla_kernel, n_chunks=n_chunks, has_state=has_state, want_state=want_state),
        grid=(batch, n_chunks),
        in_specs=in_specs,
        out_specs=out_specs,
        out_shape=out_shape,
        scratch_shapes=[pltpu.VMEM((2, GLA_QK, MIX_A), F32), pltpu.VMEM((seq_len, MIX_A), F32),
                        pltpu.VMEM((seq_len, MIX_A), F32)],
        compiler_params=_params(2),
        name="gla",
    )(*args)
    return (outs[0], outs[1]) if want_state else (outs[0], None)


DFT_ROWS = 512
FOURIER_BLOCK_ROWS = 1024
DFT_LOW = 32


def _dft_rows(rows, n):
    k = jnp.arange(n, dtype=jnp.int32)
    ang = ((rows[:, None] * k[None, :]) % n).astype(F32) * (2.0 * math.pi / n)
    return jnp.cos(ang), jnp.sin(ang)


def _dft_tables(n):
    c_lo, s_lo = _dft_rows(jnp.arange(DFT_LOW, dtype=jnp.int32), n)
    c_hi, s_hi = _dft_rows(jnp.arange(n // DFT_LOW, dtype=jnp.int32) * DFT_LOW, n)
    cos = c_hi[:, None, :] * c_lo[None, :, :] - s_hi[:, None, :] * s_lo[None, :, :]
    sin = s_hi[:, None, :] * c_lo[None, :, :] + c_hi[:, None, :] * s_lo[None, :, :]
    return cos.reshape(n, n), sin.reshape(n, n)


def _fourier_kernel(f_ref, cs_ref, cl_ref, sl_ref, out_ref, xc_ref, xs_ref, *, scale, seq_len):
    n = f_ref.shape[0]
    for g in range(MIX_C // FOURIER_GW):
        sl = slice(g * FOURIER_GW, (g + 1) * FOURIER_GW)
        x = _dot(f_ref[:, sl], cs_ref[...])
        xc_ref[:, sl] = x[:, :FOURIER_GW].astype(BF16)
        xs_ref[:, sl] = x[:, FOURIER_GW:].astype(BF16)

    block = min(DFT_ROWS, seq_len)
    passes = seq_len // block

    def body(r, carry):
        out_rows = pl.ds(pl.multiple_of(r * block, block), block)
        table_rows = pl.ds(pl.multiple_of((r % passes) * block, block), block)
        seq_rows = pl.ds(pl.multiple_of((r // passes) * seq_len, seq_len), seq_len)
        re = _dot(cl_ref[table_rows, :], xc_ref[seq_rows, :]) - _dot(sl_ref[table_rows, :], xs_ref[seq_rows, :])
        out_ref[out_rows, :] = (re * scale).astype(BF16)
        return carry

    lax.fori_loop(0, n // block, body, 0)


def _fourier_tables(seq_len):
    cc, sc = _dft_rows(jnp.arange(FOURIER_GW, dtype=jnp.int32), FOURIER_GW)
    cl, sl = _dft_tables(seq_len)
    return jnp.concatenate([cc, sc], axis=1).astype(BF16), cl.astype(BF16), sl.astype(BF16)


def _fourier(f, tables, batch, seq_len):
    n = f.shape[0]
    scale = 1.0 / math.sqrt(seq_len * FOURIER_GW)
    seqs = max(1, FOURIER_BLOCK_ROWS // seq_len)
    assert batch % seqs == 0
    rows = seqs * seq_len
    seq_spec = pl.BlockSpec((rows, MIX_C), lambda b: (b, 0))
    return pl.pallas_call(
        functools.partial(_fourier_kernel, scale=scale, seq_len=seq_len),
        grid=(batch // seqs,),
        in_specs=[seq_spec, _resident((FOURIER_GW, 2 * FOURIER_GW)),
                  _resident((seq_len, seq_len)), _resident((seq_len, seq_len))],
        out_specs=seq_spec,
        out_shape=jax.ShapeDtypeStruct((n, MIX_C), BF16),
        scratch_shapes=[pltpu.VMEM((rows, MIX_C), BF16), pltpu.VMEM((rows, MIX_C), BF16)],
        compiler_params=_params(1),
        name="fourier",
    )(f, *tables)


def _grid_pos_embed(n_tok):
    rows = n_tok // GRID_W
    quarter = D_MODEL // 4
    freqs = 1.0 / (10000.0 ** (jnp.arange(quarter, dtype=F32) / quarter))
    ar = jnp.arange(rows, dtype=F32)[:, None] * freqs
    ac = jnp.arange(GRID_W, dtype=F32)[:, None] * freqs

    def per_row(t):
        return jnp.broadcast_to(t[:, None, :], (rows, GRID_W, quarter))

    def per_col(t):
        return jnp.broadcast_to(t[None, :, :], (rows, GRID_W, quarter))

    pos = jnp.concatenate([per_row(jnp.sin(ar)), per_row(jnp.cos(ar)), per_col(jnp.sin(ac)), per_col(jnp.cos(ac))],
                          axis=-1)
    return pos.reshape(n_tok, D_MODEL)


def _prep_weights(norm_g, final_g, w_mlp1, w_mlp2, w_in_even, w_a2, b_a2, gla_norm_g, pool_w, pool_s,
                  w_out_even, w_in_odd, conv_w, conv_b, w_out_odd):
    rank_end = EVEN_MAIN + 2 * GATE_RANK
    w_even = w_in_even[:, :, :EVEN_MAIN].astype(BF16)
    w_pool_in = w_in_even[:, :, rank_end:].astype(BF16)
    w_rank = jnp.pad(w_in_even[:, :, EVEN_MAIN:rank_end].astype(BF16),
                     ((0, 0), (0, 0), (0, LANES - 2 * GATE_RANK)))
    wgate = jnp.zeros((N_EVEN, LANES, 2 * GLA_QK), F32)
    wgate = wgate.at[:, :GATE_RANK, :GLA_QK].set(w_a2[:, 0])
    wgate = wgate.at[:, GATE_RANK:2 * GATE_RANK, GLA_QK:].set(w_a2[:, 1])
    return dict(
        norm_g=norm_g.reshape(DEPTH * 2, 1, D_MODEL),
        final_g=final_g.reshape(1, D_MODEL),
        w1=w_mlp1.astype(BF16), w2=w_mlp2.astype(BF16),
        w_even=w_even, w_pool_in=w_pool_in, w_rank=w_rank, wgate=wgate.astype(BF16), bgate=b_a2.reshape(N_EVEN, 1, 2 * GLA_QK),
        gn=gla_norm_g.reshape(N_EVEN, 1, GLA_DV),
        pool_w=pool_w.astype(BF16), pool_s=pool_s.reshape(N_EVEN, 1, MIX_B),
        w_out_even=w_out_even.astype(BF16),
        w_odd=w_in_odd.astype(BF16), conv_w=conv_w, conv_b=conv_b.reshape(-1, 1, MIX_D),
        w_out_odd=w_out_odd.astype(BF16),
    )


def _trunk(x, pos, mods, gla_init, shared_mod, want_state, wp):
    batch, seq_len, _ = x.shape
    x = x.reshape(batch * seq_len, D_MODEL)
    tables = _fourier_tables(seq_len)
    states = []
    for l in range(DEPTH):
        j = l // 2
        if l % 2 == 0:
            qk, v, gate, lf, b = _in_even(x, pos, mods, wp, l, seq_len, shared_mod)
            s0 = None if gla_init is None else gla_init[:, j].reshape(batch, 2, GLA_QK, GLA_DV)
            a, s = _gla(qk, v, lf, gate, wp, l, s0, batch, seq_len, want_state)
            if want_state:
                states.append(s.reshape(batch, 2, GLA_HEADS, GLA_DK, GLA_DV))
        else:
            f, b = _in_odd(x, mods, wp, l, seq_len, shared_mod)
            a = _fourier(f, tables, batch, seq_len)
        x = _post(x, pos, a, b, mods, wp, l, seq_len, shared_mod)
        pos = None
    y = x.reshape(batch, seq_len, D_MODEL)
    return y, (jnp.stack(states, axis=1) if want_state else None)


def kernel(x_prompt, x_sample, state_gla, c, c_ctx, norm_g, final_g, w_ada, b_ada, w_mlp1, w_mlp2,
           w_in_even, w_a2, b_a2, gla_norm_g, pool_w, pool_s, w_out_even,
           w_in_odd, conv_w, conv_b, w_out_odd):
    wp = _prep_weights(norm_g, final_g, w_mlp1, w_mlp2, w_in_even, w_a2, b_a2, gla_norm_g, pool_w, pool_s,
                       w_out_even, w_in_odd, conv_w, conv_b, w_out_odd)
    n_dec = c.shape[0]
    cond = jnp.concatenate([c_ctx[None, :], c, jnp.zeros((COND_ROWS - 1 - n_dec, D_MODEL), F32)], axis=0)
    mods = _ada(cond, w_ada, b_ada).reshape(DEPTH * COND_ROWS, 1, 6 * D_MODEL)

    y_prompt, new_state = _trunk(x_prompt, None, mods, None, True, True, wp)
    pos = _grid_pos_embed(x_sample.shape[1]).astype(x_sample.dtype)
    y_sample, _ = _trunk(x_sample, pos, mods, state_gla, False, False, wp)
    return (y_prompt, y_sample, new_state.astype(x_prompt.dtype))
```
